```python
import jax, jax.numpy as jnp
from jax import lax
import numpy as np

D_MODEL = 2048
BATCH = 4
SEQ = 4096
DEPTH = 1

ATTN_HEADS = 16
ATTN_KV_HEADS = 2
ATTN_HEAD_DIM = 64
WINDOW = 128
ATTN_BLOCK = 128
ROPE_THETA = 500000.0
ROPE_DIM = ATTN_HEAD_DIM // 4
DN_HEADS = 8
DN_HEAD_K = 128
DN_HEAD_V = 128
DN_CHUNK = 64
CONV_K = 4
D_FF = 5632
NORM_EPS = 1e-6
L2_EPS = 1e-6
N_MOD = 9

ATTN_WIDTH = ATTN_HEADS * ATTN_HEAD_DIM
ATTN_KV_WIDTH = ATTN_KV_HEADS * ATTN_HEAD_DIM
DN_QK_WIDTH = DN_HEADS * DN_HEAD_K
DN_WIDTH = DN_HEADS * DN_HEAD_V
MIX_WIDTH = ATTN_WIDTH + DN_WIDTH
IN_SPLITS = (ATTN_WIDTH, ATTN_KV_WIDTH, ATTN_KV_WIDTH, DN_QK_WIDTH, DN_QK_WIDTH, DN_WIDTH, DN_WIDTH, DN_HEADS, DN_HEADS)
IN_WIDTH = sum(IN_SPLITS)
IN_OFFSETS = tuple(int(o) for o in np.cumsum(IN_SPLITS)[:-1])
CONV_CH = 2 * DN_QK_WIDTH + DN_WIDTH

kernel_name = "hymba_swa_sink_gdn_macaron_adaln"


def rms_norm(t, w):
    tf = t.astype(jnp.float32)
    y = tf * lax.rsqrt(jnp.mean(tf * tf, axis=-1, keepdims=True) + NORM_EPS) * w.astype(jnp.float32)
    return y.astype(t.dtype)


def modulate(t, shift, scale):
    return t * (1 + scale[:, None, :]) + shift[:, None, :]


def swiglu(u, w_gate, w_up, w_down):
    return (jax.nn.silu(u @ w_gate) * (u @ w_up)) @ w_down


def rope_tables(positions):
    inv_freq = ROPE_THETA ** (-jnp.arange(0, ROPE_DIM, 2, dtype=jnp.float32) / ROPE_DIM)
    ang = positions.astype(jnp.float32)[..., None] * inv_freq
    return jnp.cos(ang), jnp.sin(ang)


def apply_partial_rope(t, cos, sin):
    half = ROPE_DIM // 2
    cos = cos[:, :, None, :]
    sin = sin[:, :, None, :]
    t1 = t[..., :half]
    t2 = t[..., half:ROPE_DIM]
    return jnp.concatenate([t1 * cos - t2 * sin, t2 * cos + t1 * sin, t[..., ROPE_DIM:]], axis=-1)


def sliding_window_sink_attention(q, k, v, cos, sin, sinks):
    B, S, _ = q.shape
    nb = S // ATTN_BLOCK
    G = ATTN_HEADS // ATTN_KV_HEADS
    f32 = jnp.float32
    q = apply_partial_rope(q.astype(f32).reshape(B, S, ATTN_HEADS, ATTN_HEAD_DIM), cos, sin)
    k = apply_partial_rope(k.astype(f32).reshape(B, S, ATTN_KV_HEADS, ATTN_HEAD_DIM), cos, sin)
    v = v.astype(f32).reshape(B, S, ATTN_KV_HEADS, ATTN_HEAD_DIM)
    qb = q.reshape(B, nb, ATTN_BLOCK, ATTN_KV_HEADS, G, ATTN_HEAD_DIM)

    def band(t):
        tb = t.reshape(B, nb, ATTN_BLOCK, ATTN_KV_HEADS, ATTN_HEAD_DIM)
        prev = jnp.pad(tb, ((0, 0), (1, 0), (0, 0), (0, 0), (0, 0)))[:, :-1]
        return jnp.concatenate([prev, tb], axis=2)

    kb, vb = band(k), band(v)
    blk = jnp.arange(nb)[:, None, None] * ATTN_BLOCK
    q_pos = blk + jnp.arange(ATTN_BLOCK)[None, :, None]
    k_pos = blk - ATTN_BLOCK + jnp.arange(2 * ATTN_BLOCK)[None, None, :]
    valid = (k_pos <= q_pos) & (k_pos > q_pos - WINDOW) & (k_pos >= 0)
    scores = jnp.einsum('bnqhgd,bnkhd->bnhgqk', qb, kb) * (ATTN_HEAD_DIM ** -0.5)
    scores = jnp.where(valid[None, :, None, None], scores, -jnp.inf)
    sink = sinks.astype(f32).reshape(1, 1, ATTN_KV_HEADS, G, 1, 1)
    m = jnp.maximum(scores.max(axis=-1, keepdims=True), sink)
    p = jnp.exp(scores - m)
    probs = p / (p.sum(axis=-1, keepdims=True) + jnp.exp(sink - m))
    out = jnp.einsum('bnhgqk,bnkhd->bnqhgd', probs, vb)
    return out.reshape(B, S, ATTN_WIDTH)


def causal_short_conv(z, w):
    y = lax.conv_general_dilated(z, w[:, None, :].astype(z.dtype), window_strides=(1,),
                                 padding=[(CONV_K - 1, 0)],
                                 dimension_numbers=('NWC', 'WIO', 'NWC'),
                                 feature_group_count=z.shape[-1])
    return jax.nn.silu(y)


def l2_normalize(t):
    return t * lax.rsqrt(jnp.sum(t * t, axis=-1, keepdims=True) + L2_EPS)


def chunk_gated_delta_rule(q, k, v, g, beta):
    B, S, H, dk = q.shape
    dv = v.shape[-1]
    C = DN_CHUNK
    n = S // C

    def chunks(t):
        return t.reshape(B, n, C, H, -1).transpose(0, 3, 1, 2, 4)

    q, k, v = chunks(q), chunks(k), chunks(v)
    beta = beta.reshape(B, n, C, H).transpose(0, 3, 1, 2)
    g = jnp.cumsum(g.reshape(B, n, C, H).transpose(0, 3, 1, 2), axis=-1)
    tri = jnp.tril(jnp.ones((C, C), dtype=bool))
    strict = jnp.tril(jnp.ones((C, C), dtype=bool), -1)
    decay = jnp.exp(jnp.where(tri, g[..., :, None] - g[..., None, :], -jnp.inf))
    k_beta = k * beta[..., None]
    v_beta = v * beta[..., None]
    A = jnp.where(strict, jnp.einsum('bhncd,bhnsd->bhncs', k_beta, k) * decay, 0.0)
    eye = jnp.eye(C, dtype=q.dtype)
    T = lax.linalg.triangular_solve(A + eye, jnp.broadcast_to(eye, A.shape), left_side=True,
                                    lower=True, unit_diagonal=True)
    u = jnp.einsum('bhncs,bhnsd->bhncd', T, v_beta)
    w = jnp.einsum('bhncs,bhnsd->bhncd', T, k_beta * jnp.exp(g)[..., None])
    a_intra = jnp.where(tri, jnp.einsum('bhncd,bhnsd->bhncs', q, k) * decay, 0.0)
    q_g = q * jnp.exp(g)[..., None]
    g_last = g[..., -1]
    k_dec = k * jnp.exp(g_last[..., None] - g)[..., None]

    def step(state, inp):
        u_i, w_i, qg_i, a_i, kd_i, gl_i = inp
        v_new = u_i - jnp.einsum('bhck,bhkv->bhcv', w_i, state)
        o_i = jnp.einsum('bhck,bhkv->bhcv', qg_i, state) + jnp.einsum('bhcs,bhsv->bhcv', a_i, v_new)
        state = state * jnp.exp(gl_i)[..., None, None] + jnp.einsum('bhck,bhcv->bhkv', kd_i, v_new)
        return state, o_i

    xs = (jnp.moveaxis(u, 2, 0), jnp.moveaxis(w, 2, 0), jnp.moveaxis(q_g, 2, 0),
          jnp.moveaxis(a_intra, 2, 0), jnp.moveaxis(k_dec, 2, 0), jnp.moveaxis(g_last, 2, 0))
    state0 = jnp.zeros((B, H, dk, dv), dtype=q.dtype)
    _, o = lax.scan(step, state0, xs)
    return o.transpose(1, 0, 3, 2, 4).reshape(B, S, H, dv)


def gated_deltanet(q, k, v, gate, a, b, conv_w, a_log, dt_bias, norm_w):
    B, S, _ = q.shape
    f32 = jnp.float32
    qkv = causal_short_conv(jnp.concatenate([q, k, v], axis=-1), conv_w).astype(f32)
    q, k, v = jnp.split(qkv, [DN_QK_WIDTH, 2 * DN_QK_WIDTH], axis=-1)
    q = l2_normalize(q.reshape(B, S, DN_HEADS, DN_HEAD_K)) * (DN_HEAD_K ** -0.5)
    k = l2_normalize(k.reshape(B, S, DN_HEADS, DN_HEAD_K))
    v = v.reshape(B, S, DN_HEADS, DN_HEAD_V)
    beta = jax.nn.sigmoid(b.astype(f32))
    g = -jnp.exp(a_log.astype(f32)) * jax.nn.softplus(a.astype(f32) + dt_bias.astype(f32))
    o = chunk_gated_delta_rule(q, k, v, g, beta)
    o = rms_norm(o, norm_w) * jax.nn.silu(gate.astype(f32).reshape(B, S, DN_HEADS, DN_HEAD_V))
    return o.reshape(B, S, DN_WIDTH)


def hybrid_mixer(u, cos, sin, w_in, conv_w, a_log, dt_bias, attn_sinks, dn_norm_w, w_out):
    proj = u @ w_in
    aq, ak, av, dq, dk, dv, dg, da, db = jnp.split(proj, IN_OFFSETS, axis=-1)
    attn_out = sliding_window_sink_attention(aq, ak, av, cos, sin, attn_sinks)
    dn_out = gated_deltanet(dq, dk, dv, dg, da, db, conv_w, a_log, dt_bias, dn_norm_w)
    mixed = jnp.concatenate([attn_out, dn_out], axis=-1).astype(u.dtype)
    return mixed @ w_out


def setup_inputs(seed: int = 0) -> dict:
    key = jax.random.key(seed)
    ks = jax.random.split(key, 24)
    f32 = jnp.float32
    L, D = DEPTH, D_MODEL

    def dense(k, shape, fan_in):
        return jax.random.normal(k, shape, f32) * fan_in ** -0.5

    def gain(k, shape):
        return 1.0 + 0.02 * jax.random.normal(k, shape, f32)

    dt = jnp.exp(jax.random.uniform(ks[12], (L, DN_HEADS), f32, minval=np.log(0.001), maxval=np.log(0.1)))
    return {
        "x": jax.random.normal(ks[0], (BATCH, SEQ, D), f32),
        "c": jax.random.normal(ks[1], (BATCH, D), f32),
        "positions": jnp.tile(jnp.arange(SEQ, dtype=jnp.int32)[None, :], (BATCH, 1)),
        "ada_w": dense(ks[2], (L, D, N_MOD * D), D),
        "ada_b": 0.02 * jax.random.normal(ks[3], (L, N_MOD * D), f32),
        "norm_ffn1": gain(ks[4], (L, D)),
        "ffn1_w_gate": dense(ks[5], (L, D, D_FF), D),
        "ffn1_w_up": dense(ks[6], (L, D, D_FF), D),
        "ffn1_w_down": dense(ks[7], (L, D_FF, D), D_FF),
        "norm_mix": gain(ks[8], (L, D)),
        "w_in": dense(ks[9], (L, D, IN_WIDTH), D),
        "conv_w": dense(ks[10], (L, CONV_K, CONV_CH), CONV_K),
        "a_log": jnp.log(jax.random.uniform(ks[11], (L, DN_HEADS), f32, minval=1.0, maxval=16.0)),
        "dt_bias": dt + jnp.log(-jnp.expm1(-dt)),
        "attn_sinks": 0.5 * jax.random.normal(ks[13], (L, ATTN_HEADS), f32),
        "dn_norm_w": gain(ks[14], (L, DN_HEAD_V)),
        "w_out": dense(ks[15], (L, MIX_WIDTH, D), MIX_WIDTH),
        "norm_ffn2": gain(ks[16], (L, D)),
        "ffn2_w_gate": dense(ks[17], (L, D, D_FF), D),
        "ffn2_w_up": dense(ks[18], (L, D, D_FF), D),
        "ffn2_w_down": dense(ks[19], (L, D_FF, D), D_FF),
        "final_norm": gain(ks[20], (D,)),
    }


def reference(x, c, positions, ada_w, ada_b, norm_ffn1, ffn1_w_gate, ffn1_w_up, ffn1_w_down,
              norm_mix, w_in, conv_w, a_log, dt_bias, attn_sinks, dn_norm_w, w_out,
              norm_ffn2, ffn2_w_gate, ffn2_w_up, ffn2_w_down, final_norm):
    cos, sin = rope_tables(positions)
    c_act = jax.nn.silu(c)
    h = x
    for l in range(DEPTH):
        mod = c_act @ ada_w[l] + ada_b[l]
        sh1, sc1, g1, sh2, sc2, g2, sh3, sc3, g3 = jnp.split(mod, N_MOD, axis=-1)
        u = modulate(rms_norm(h, norm_ffn1[l]), sh1, sc1)
        h = h + 0.5 * g1[:, None, :] * swiglu(u, ffn1_w_gate[l], ffn1_w_up[l], ffn1_w_down[l])
        u = modulate(rms_norm(h, norm_mix[l]), sh2, sc2)
        h = h + g2[:, None, :] * hybrid_mixer(u, cos, sin, w_in[l], conv_w[l], a_log[l], dt_bias[l],
                                              attn_sinks[l], dn_norm_w[l], w_out[l])
        u = modulate(rms_norm(h, norm_ffn2[l]), sh3, sc3)
        h = h + 0.5 * g3[:, None, :] * swiglu(u, ffn2_w_gate[l], ffn2_w_up[l], ffn2_w_down[l])
    return rms_norm(h, final_norm)
```

```python
import functools

import numpy as np
import jax
import jax.numpy as jnp
from jax import lax
from jax.experimental import pallas as pl
from jax.experimental.pallas import tpu as pltpu

F32 = jnp.float32
BF16 = jnp.bfloat16

ATTN_HEADS = 16
ATTN_KV_HEADS = 2
ATTN_HEAD_DIM = 64
WINDOW = 128
ATTN_BLOCK = 128
ROPE_THETA = 500000.0
ROPE_DIM = ATTN_HEAD_DIM // 4
DN_HEADS = 8
DN_HEAD_K = 128
DN_HEAD_V = 128
DN_CHUNK = 64
CONV_K = 4
NORM_EPS = 1e-6
L2_EPS = 1e-6
N_MOD = 9

ATTN_WIDTH = ATTN_HEADS * ATTN_HEAD_DIM
ATTN_KV_WIDTH = ATTN_KV_HEADS * ATTN_HEAD_DIM
DN_WIDTH = DN_HEADS * DN_HEAD_V
IN_WIDTH = ATTN_WIDTH + 2 * ATTN_KV_WIDTH + 4 * DN_WIDTH + 2 * DN_HEADS
COL_DN_QKV = 0
COL_DN_GATE = 3 * DN_WIDTH
COL_ATTN_Q = 4 * DN_WIDTH
COL_ATTN_K = COL_ATTN_Q + ATTN_WIDTH
COL_ATTN_V = COL_ATTN_K + ATTN_KV_WIDTH
COL_DN_AB = COL_ATTN_V + ATTN_KV_WIDTH
LANES = 128
SUBLANES = 8
VMEM_LIMIT = 56 * 1024 * 1024
TOKEN_TILE = 512
FFN_TF = 512
MIX_IN_TN = 512
ADALN_TN = 1024


def _cparams(sem):
    return pltpu.CompilerParams(dimension_semantics=sem, vmem_limit_bytes=VMEM_LIMIT)


def _silu(t):
    return t * jax.nn.sigmoid(t)


def _norm_mod(h, nw, shift, scale):
    var = jnp.mean(h * h, axis=-1, keepdims=True)
    y = h * lax.rsqrt(var + NORM_EPS) * nw
    return y * (1.0 + scale) + shift


def _adaln_body(c_ref, w_ref, b_ref, o_ref):
    ca = _silu(c_ref[...]).astype(BF16)
    o_ref[...] = jnp.dot(ca, w_ref[...].astype(BF16), preferred_element_type=F32) + b_ref[...]


def _adaln(c, w, b):
    tn = ADALN_TN
    nb, d = c.shape
    n = w.shape[1]
    rows = -(-nb // SUBLANES) * SUBLANES
    c_pad = jnp.pad(c, ((0, rows - nb), (0, 0)))
    out = pl.pallas_call(
        _adaln_body,
        grid=(n // tn,),
        in_specs=[pl.BlockSpec((rows, d), lambda j: (0, 0)),
                  pl.BlockSpec((d, tn), lambda j: (0, j)),
                  pl.BlockSpec((1, tn), lambda j: (0, j))],
        out_specs=pl.BlockSpec((rows, tn), lambda j: (0, j)),
        out_shape=jax.ShapeDtypeStruct((rows, n), F32),
        compiler_params=_cparams(("arbitrary",)),
        name="adaln",
    )(c_pad, w, b.reshape(1, n))
    return out[:nb]


def _ffn_body(h_ref, nw_ref, sh_ref, sc_ref, g_ref, wg_ref, wu_ref, wd_ref, fn_ref,
              o_ref, u_scr, acc_scr, *, final_norm):
    j = pl.program_id(1)

    @pl.when(j == 0)
    def _():
        u = _norm_mod(h_ref[...], nw_ref[...], sh_ref[0], sc_ref[0])
        u_scr[...] = u.astype(BF16)
        acc_scr[...] = jnp.zeros_like(acc_scr)

    u = u_scr[...]
    gate = jnp.dot(u, wg_ref[...], preferred_element_type=F32)
    up = jnp.dot(u, wu_ref[...], preferred_element_type=F32)
    act = (_silu(gate) * up).astype(BF16)
    acc_scr[...] += jnp.dot(act, wd_ref[...], preferred_element_type=F32)

    @pl.when(j == pl.num_programs(1) - 1)
    def _():
        hn = h_ref[...] + 0.5 * g_ref[0] * acc_scr[...]
        if final_norm:
            var = jnp.mean(hn * hn, axis=-1, keepdims=True)
            hn = hn * lax.rsqrt(var + NORM_EPS) * fn_ref[...]
        o_ref[...] = hn


def _ffn(h, mod3, seq, k0, nw, wg, wu, wd, fn, final_norm):
    tm, tf = TOKEN_TILE, FFN_TF
    t, d = h.shape
    dff = wg.shape[1]
    mod_spec = lambda k: pl.BlockSpec((1, 1, d), lambda i, j: (((i * tm) // seq) * N_MOD + k, 0, 0))
    return pl.pallas_call(
        functools.partial(_ffn_body, final_norm=final_norm),
        grid=(t // tm, dff // tf),
        in_specs=[pl.BlockSpec((tm, d), lambda i, j: (i, 0)),
                  pl.BlockSpec((1, d), lambda i, j: (0, 0)),
                  mod_spec(k0), mod_spec(k0 + 1), mod_spec(k0 + 2),
                  pl.BlockSpec((d, tf), lambda i, j: (0, j)),
                  pl.BlockSpec((d, tf), lambda i, j: (0, j)),
                  pl.BlockSpec((tf, d), lambda i, j: (j, 0)),
                  pl.BlockSpec((1, d), lambda i, j: (0, 0))],
        out_specs=pl.BlockSpec((tm, d), lambda i, j: (i, 0)),
        out_shape=jax.ShapeDtypeStruct((t, d), F32),
        scratch_shapes=[pltpu.VMEM((tm, d), BF16), pltpu.VMEM((tm, d), F32)],
        compiler_params=_cparams(("arbitrary", "arbitrary")),
        name="ffn_final" if final_norm else "ffn",
    )(h, nw.reshape(1, d), mod3, mod3, mod3, wg, wu, wd, fn.reshape(1, d))


def _mix_in_body(h_ref, nw_ref, sh_ref, sc_ref, w_ref, wabt_ref, o_ref, abt_ref, u_scr):
    j = pl.program_id(1)

    @pl.when(j == 0)
    def _():
        u = _norm_mod(h_ref[...], nw_ref[...], sh_ref[0], sc_ref[0]).astype(BF16)
        u_scr[...] = u
        abt_ref[...] = lax.dot_general(wabt_ref[...], u, (((1,), (1,)), ((), ())),
                                       preferred_element_type=F32)

    o_ref[...] = jnp.dot(u_scr[...], w_ref[...], preferred_element_type=F32)


def _mix_in(h, mod3, seq, nw, w_pad, wab_t):
    tm, tn = TOKEN_TILE, MIX_IN_TN
    t, d = h.shape
    n = w_pad.shape[1]
    nab = wab_t.shape[0]
    mod_spec = lambda k: pl.BlockSpec((1, 1, d), lambda i, j: (((i * tm) // seq) * N_MOD + k, 0, 0))
    return pl.pallas_call(
        _mix_in_body,
        grid=(t // tm, n // tn),
        in_specs=[pl.BlockSpec((tm, d), lambda i, j: (i, 0)),
                  pl.BlockSpec((1, d), lambda i, j: (0, 0)),
                  mod_spec(3), mod_spec(4),
                  pl.BlockSpec((d, tn), lambda i, j: (0, j)),
                  pl.BlockSpec((nab, d), lambda i, j: (0, 0))],
        out_specs=[pl.BlockSpec((tm, tn), lambda i, j: (i, j)),
                   pl.BlockSpec((nab, tm), lambda i, j: (0, i))],
        out_shape=[jax.ShapeDtypeStruct((t, n), F32),
                   jax.ShapeDtypeStruct((nab, t), F32)],
        scratch_shapes=[pltpu.VMEM((tm, d), BF16)],
        compiler_params=_cparams(("arbitrary", "arbitrary")),
        name="mix_in",
    )(h, nw.reshape(1, d), mod3, mod3, w_pad, wab_t)


def _rope(t, cos, sin_signed, low_half):
    width = t.shape[-1]
    half = ROPE_DIM // 2
    partner = jnp.where(low_half, pltpu.roll(t, width - half, axis=1), pltpu.roll(t, half, axis=1))
    return t * cos + partner * sin_signed


def _attn_body(sink_ref, pos_ref, invf_ref, q_ref, k_ref, v_ref, o_ref, kprev, vprev):
    n = pl.program_id(1)
    blk = ATTN_BLOCK
    hd = ATTN_HEAD_DIM
    group = ATTN_HEADS // ATTN_KV_HEADS

    @pl.when(n == 0)
    def _():
        kprev[...] = jnp.zeros_like(kprev)
        vprev[...] = jnp.zeros_like(vprev)

    ang = pos_ref[...].astype(F32) * invf_ref[...]
    lane = lax.broadcasted_iota(jnp.int32, (blk, LANES), 1) % hd
    low = lane < (ROPE_DIM // 2)
    cos = jnp.cos(ang)
    sin = jnp.sin(ang)
    sin_signed = jnp.where(low, -sin, sin)

    k = _rope(k_ref[...], cos, sin_signed, low)
    reps = ATTN_WIDTH // LANES
    cos_q = jnp.concatenate([cos] * reps, axis=1)
    sin_q = jnp.concatenate([sin_signed] * reps, axis=1)
    low_q = jnp.concatenate([low] * reps, axis=1)
    q = _rope(q_ref[...], cos_q, sin_q, low_q) * (hd ** -0.5)
    q = q.astype(BF16)

    kband = jnp.concatenate([kprev[...], k.astype(BF16)], axis=0)
    v = v_ref[...].astype(BF16)
    vband = jnp.concatenate([vprev[...], v], axis=0)

    qi = lax.broadcasted_iota(jnp.int32, (blk, 2 * blk), 0)
    kj = lax.broadcasted_iota(jnp.int32, (blk, 2 * blk), 1)
    valid = (kj > qi + (blk - WINDOW)) & (kj <= qi + blk) & ((kj >= blk) | (n > 0))

    for h in range(ATTN_HEADS):
        hk = h // group
        qh = q[:, h * hd:(h + 1) * hd]
        kh = kband[:, hk * hd:(hk + 1) * hd]
        vh = vband[:, hk * hd:(hk + 1) * hd]
        s = lax.dot_general(qh, kh, (((1,), (1,)), ((), ())), preferred_element_type=F32)
        s = jnp.where(valid, s, -jnp.inf)
        sink = sink_ref[h]
        m = jnp.maximum(jnp.max(s, axis=-1, keepdims=True), sink)
        p = jnp.exp(s - m)
        denom = jnp.sum(p, axis=-1, keepdims=True) + jnp.exp(sink - m)
        o = jnp.dot(p.astype(BF16), vh, preferred_element_type=F32) / denom
        o_ref[:, h * hd:(h + 1) * hd] = o.astype(o_ref.dtype)

    kprev[...] = k.astype(BF16)
    vprev[...] = v


def _attention(proj, positions, sinks, batch, seq):
    t = batch * seq
    nb = seq // ATTN_BLOCK
    blk = ATTN_BLOCK
    half = ROPE_DIM // 2
    inv_freq = ROPE_THETA ** (-jnp.arange(0, ROPE_DIM, 2, dtype=F32) / ROPE_DIM)
    head_pat = jnp.concatenate([inv_freq, inv_freq, jnp.zeros((ATTN_HEAD_DIM - 2 * half,), F32)])
    invf = jnp.tile(head_pat, LANES // ATTN_HEAD_DIM).reshape(1, LANES)
    kcol = COL_ATTN_K // LANES
    row = lambda b, n: b * nb + n
    return pl.pallas_call(
        _attn_body,
        grid=(batch, nb),
        in_specs=[pl.BlockSpec(memory_space=pltpu.SMEM),
                  pl.BlockSpec((blk, 1), lambda b, n: (row(b, n), 0)),
                  pl.BlockSpec((1, LANES), lambda b, n: (0, 0)),
                  pl.BlockSpec((blk, ATTN_WIDTH), lambda b, n: (row(b, n), COL_ATTN_Q // ATTN_WIDTH)),
                  pl.BlockSpec((blk, LANES), lambda b, n: (row(b, n), kcol)),
                  pl.BlockSpec((blk, LANES), lambda b, n: (row(b, n), kcol + 1))],
        out_specs=pl.BlockSpec((blk, ATTN_WIDTH), lambda b, n: (row(b, n), 0)),
        out_shape=jax.ShapeDtypeStruct((t, ATTN_WIDTH), BF16),
        scratch_shapes=[pltpu.VMEM((blk, LANES), BF16), pltpu.VMEM((blk, LANES), BF16)],
        compiler_params=_cparams(("arbitrary", "arbitrary")),
        name="attn",
    )(sinks, positions.reshape(t, 1), invf, proj, proj, proj)


def _split_bf16(t):
    hi = t.astype(BF16)
    lo = (t - hi.astype(F32)).astype(BF16)
    return hi, lo


def _dot3(a, b):
    ah, al = _split_bf16(a)
    bh, bl = _split_bf16(b)
    n = b.shape[1]
    first = jnp.dot(ah, jnp.concatenate([bh, bl], axis=1), preferred_element_type=F32)
    return first[:, :n] + first[:, n:] + jnp.dot(al, bh, preferred_element_type=F32)


INV_BASE = 8


def _inverse_masks(r, s, c):
    same = lambda bs: (r // bs) == (s // bs)
    masks = [same(INV_BASE)]
    bs = INV_BASE
    while bs < c:
        masks.append(same(2 * bs) & jnp.logical_not(same(bs)))
        bs *= 2
    return masks


def _unit_lower_inverse(a, eye, masks):
    c = a.shape[0]
    d = jnp.where(masks[0], a, 0.0)
    p = eye - d
    x = _dot3(d, d)
    both = _dot3(jnp.concatenate([x, p], axis=0), x)
    p = p + both[c:]
    p = p + _dot3(p, both[:c])
    for m in masks[1:]:
        p = p - _dot3(_dot3(p, jnp.where(m, a, 0.0)), p)
    return p


def _softplus(t):
    return jnp.maximum(t, 0.0) + jnp.log(1.0 + jnp.exp(-jnp.abs(t)))


def _split3_bf16(t):
    hi = t.astype(BF16)
    r1 = t - hi.astype(F32)
    mid = r1.astype(BF16)
    lo = (r1 - mid.astype(F32)).astype(BF16)
    return hi, mid, lo


def _cumsum_rows(lower, t):
    n = t.shape[1]
    parts = jnp.dot(lower.astype(BF16), jnp.concatenate(_split3_bf16(t), axis=1),
                    preferred_element_type=F32)
    return parts[:, :n] + parts[:, n:2 * n] + parts[:, 2 * n:]


def _cumsum_lanes(t, upper):
    m = t.shape[0]
    parts = jnp.dot(jnp.concatenate(_split3_bf16(t), axis=0), upper.astype(BF16),
                    preferred_element_type=F32)
    return parts[:m] + parts[m:2 * m] + parts[2 * m:]


def _dn_body(qkv_ref, gate_ref, ab_ref, abt_ref, cw_ref, alog_ref, dtb_ref,
             alogt_ref, dtbt_ref, nw_ref, o_ref, zbuf, state):
    ci = pl.program_id(1)
    c = DN_CHUNK
    hk = DN_HEAD_K
    width = DN_WIDTH
    pad = SUBLANES
    hist = CONV_K - 1

    @pl.when(ci == 0)
    def _():
        state[...] = jnp.zeros_like(state)
        zbuf[0:pad, :] = jnp.zeros((pad, 3 * width), F32)

    zbuf[pad:pad + c, :] = qkv_ref[...]
    y = jnp.zeros((c, 3 * width), F32)
    for j in range(CONV_K):
        y = y + zbuf[pad - hist + j:pad - hist + j + c, :] * cw_ref[j:j + 1, :]
    zbuf[pad - hist:pad, :] = zbuf[pad + c - hist:pad + c, :]
    y = _silu(y)

    ab = ab_ref[...]
    g_col_raw = -jnp.exp(alog_ref[...]) * _softplus(ab + dtb_ref[...])
    beta_col = jax.nn.sigmoid(ab)
    r = lax.broadcasted_iota(jnp.int32, (c, c), 0)
    s = lax.broadcasted_iota(jnp.int32, (c, c), 1)
    tri = r >= s
    strict = r > s
    eye = jnp.where(r == s, 1.0, 0.0).astype(F32)
    inv_masks = _inverse_masks(r, s, c)
    lower = jnp.where(tri, 1.0, 0.0).astype(F32)
    upper = jnp.where(r <= s, 1.0, 0.0).astype(F32)
    g_col = _cumsum_rows(lower, g_col_raw)
    g_row_raw = -jnp.exp(alogt_ref[...]) * _softplus(abt_ref[0] + dtbt_ref[...])
    g_row = _cumsum_lanes(g_row_raw, upper)

    for h in range(DN_HEADS):
        hs = slice(h * hk, (h + 1) * hk)
        qh = y[:, hs]
        kh = y[:, width + h * hk:width + (h + 1) * hk]
        vh = y[:, 2 * width + h * hk:2 * width + (h + 1) * hk]
        qh = qh * lax.rsqrt(jnp.sum(qh * qh, axis=-1, keepdims=True) + L2_EPS) * (hk ** -0.5)
        kh = kh * lax.rsqrt(jnp.sum(kh * kh, axis=-1, keepdims=True) + L2_EPS)
        beta = beta_col[:, DN_HEADS + h:DN_HEADS + h + 1]
        gc = g_col[:, h:h + 1]
        gr = g_row[h:h + 1, :]
        eg = jnp.exp(gc)
        glast = gc[c - 1:c, :]
        kb = kh * beta
        vb = vh * beta
        kbg = kb * eg
        qg = qh * eg
        kd = kh * jnp.exp(glast - gc)
        decay = jnp.exp(jnp.where(tri, gc - gr, -jnp.inf))

        khb = kh.astype(BF16)
        lhs = jnp.concatenate([kb, qh], axis=0).astype(BF16)
        kq = lax.dot_general(lhs, khb, (((1,), (1,)), ((), ())), preferred_element_type=F32)
        a = jnp.where(strict, kq[:c] * decay, 0.0)
        a_intra = jnp.where(tri, kq[c:] * decay, 0.0)
        t_inv = _unit_lower_inverse(a, eye, inv_masks)
        uw = jnp.dot(t_inv.astype(BF16), jnp.concatenate([vb, kbg], axis=1).astype(BF16),
                     preferred_element_type=F32)
        u = uw[:, :DN_HEAD_V]
        w = uw[:, DN_HEAD_V:]

        st = state[h]
        ws = jnp.dot(jnp.concatenate([w, qg], axis=0).astype(BF16), st.astype(BF16),
                     preferred_element_type=F32)
        v_new = u - ws[:c]
        v_new_b = v_new.astype(BF16)
        o = ws[c:] + jnp.dot(a_intra.astype(BF16), v_new_b, preferred_element_type=F32)
        state[h] = st * jnp.exp(glast) + lax.dot_general(
            kd.astype(BF16), v_new_b, (((0,), (0,)), ((), ())), preferred_element_type=F32)

        var = jnp.mean(o * o, axis=-1, keepdims=True)
        on = o * lax.rsqrt(var + NORM_EPS) * nw_ref[...]
        o_ref[:, hs] = (on * _silu(gate_ref[:, hs])).astype(o_ref.dtype)


def _deltanet(proj, abt, conv_w, a_log, dt_bias, norm_w, batch, seq):
    t = batch * seq
    c = DN_CHUNK
    nc = seq // c
    width = DN_WIDTH
    row = lambda b, i: b * nc + i
    nh2 = 2 * DN_HEADS
    abt3 = abt.reshape(nh2, t // c, c).transpose(1, 0, 2)
    lane_pad = lambda p: jnp.pad(p.reshape(1, -1), ((0, 0), (0, LANES - p.shape[0])))
    col = lambda p: jnp.pad(p.reshape(-1, 1), ((0, nh2 - p.shape[0]), (0, 0)))
    const = lambda shape: pl.BlockSpec(shape, lambda b, i: (0,) * len(shape))
    return pl.pallas_call(
        _dn_body,
        grid=(batch, nc),
        in_specs=[pl.BlockSpec((c, 3 * width), lambda b, i: (row(b, i), COL_DN_QKV // (3 * width))),
                  pl.BlockSpec((c, width), lambda b, i: (row(b, i), COL_DN_GATE // width)),
                  pl.BlockSpec((c, LANES), lambda b, i: (row(b, i), COL_DN_AB // LANES)),
                  pl.BlockSpec((1, nh2, c), lambda b, i: (row(b, i), 0, 0)),
                  const((CONV_K, 3 * width)),
                  const((1, LANES)), const((1, LANES)),
                  const((nh2, 1)), const((nh2, 1)),
                  const((1, DN_HEAD_V))],
        out_specs=pl.BlockSpec((c, width), lambda b, i: (row(b, i), 0)),
        out_shape=jax.ShapeDtypeStruct((t, width), BF16),
        scratch_shapes=[pltpu.VMEM((SUBLANES + c, 3 * width), F32),
                        pltpu.VMEM((DN_HEADS, DN_HEAD_K, DN_HEAD_V), F32)],
        compiler_params=_cparams(("arbitrary", "arbitrary")),
        name="dn",
    )(proj, proj, proj, abt3, conv_w, lane_pad(a_log), lane_pad(dt_bias),
      col(a_log), col(dt_bias), norm_w.reshape(1, -1))


def _mix_out_body(h_ref, g_ref, a_ref, d_ref, wa_ref, wd_ref, o_ref):
    mixed = jnp.dot(a_ref[...], wa_ref[...], preferred_element_type=F32)
    mixed = mixed + jnp.dot(d_ref[...], wd_ref[...], preferred_element_type=F32)
    o_ref[...] = h_ref[...] + g_ref[0] * mixed


def _mix_out(h, mod3, seq, attn_out, dn_out, w_out):
    tm = TOKEN_TILE
    t, d = h.shape
    wa = w_out[:ATTN_WIDTH]
    wd = w_out[ATTN_WIDTH:]
    return pl.pallas_call(
        _mix_out_body,
        grid=(t // tm,),
        in_specs=[pl.BlockSpec((tm, d), lambda i: (i, 0)),
                  pl.BlockSpec((1, 1, d), lambda i: (((i * tm) // seq) * N_MOD + 5, 0, 0)),
                  pl.BlockSpec((tm, ATTN_WIDTH), lambda i: (i, 0)),
                  pl.BlockSpec((tm, DN_WIDTH), lambda i: (i, 0)),
                  pl.BlockSpec((ATTN_WIDTH, d), lambda i: (0, 0)),
                  pl.BlockSpec((DN_WIDTH, d), lambda i: (0, 0))],
        out_specs=pl.BlockSpec((tm, d), lambda i: (i, 0)),
        out_shape=jax.ShapeDtypeStruct((t, d), F32),
        compiler_params=_cparams(("arbitrary",)),
        name="mix_out",
    )(h, mod3, attn_out, dn_out, wa, wd)


def kernel(x, c, positions, ada_w, ada_b, norm_ffn1, ffn1_w_gate, ffn1_w_up, ffn1_w_down, norm_mix, w_in, conv_w, a_log, dt_bias, attn_sinks, dn_norm_w, w_out, norm_ffn2, ffn2_w_gate, ffn2_w_up, ffn2_w_down, final_norm):
    batch, seq, d = x.shape
    depth = ada_w.shape[0]
    assert depth >= 1 and seq % ATTN_BLOCK == 0 and seq % DN_CHUNK == 0 and seq % TOKEN_TILE == 0
    t = batch * seq
    h = x.reshape(t, d)
    o_ak, o_av, o_dq = ATTN_WIDTH, ATTN_WIDTH + ATTN_KV_WIDTH, ATTN_WIDTH + 2 * ATTN_KV_WIDTH
    o_dg, o_ab = o_dq + 3 * DN_WIDTH, o_dq + 4 * DN_WIDTH
    n_pad = -(-IN_WIDTH // MIX_IN_TN) * MIX_IN_TN
    for l in range(depth):
        mod3 = _adaln(c, ada_w[l], ada_b[l]).reshape(batch * N_MOD, 1, d)
        h = _ffn(h, mod3, seq, 0, norm_ffn1[l], ffn1_w_gate[l].astype(BF16), ffn1_w_up[l].astype(BF16),
                 ffn1_w_down[l].astype(BF16), final_norm, False)
        w = w_in[l].astype(BF16)
        w_perm = jnp.concatenate([w[:, o_dq:o_ab], w[:, :o_dq], w[:, o_ab:],
                                  jnp.zeros((d, n_pad - IN_WIDTH), BF16)], axis=1)
        proj, abt = _mix_in(h, mod3, seq, norm_mix[l], w_perm, w[:, o_ab:].T)
        attn_out = _attention(proj, positions, attn_sinks[l], batch, seq)
        dn_out = _deltanet(proj, abt, conv_w[l], a_log[l], dt_bias[l], dn_norm_w[l], batch, seq)
        h = _mix_out(h, mod3, seq, attn_out, dn_out, w_out[l].astype(BF16))
        h = _ffn(h, mod3, seq, 6, norm_ffn2[l], ffn2_w_gate[l].astype(BF16), ffn2_w_up[l].astype(BF16),
                 ffn2_w_down[l].astype(BF16), final_norm, l == depth - 1)
    return h.reshape(batch, seq, d)
```

```python
import functools

import numpy as np
import jax
import jax.numpy as jnp
from jax import lax
from jax.experimental import pallas as pl
from jax.experimental.pallas import tpu as pltpu

F32 = jnp.float32
BF16 = jnp.bfloat16

ATTN_HEADS = 16
ATTN_KV_HEADS = 2
ATTN_HEAD_DIM = 64
WINDOW = 128
ATTN_BLOCK = 128
ROPE_THETA = 500000.0
ROPE_DIM = ATTN_HEAD_DIM // 4
DN_HEADS = 8
DN_HEAD_K = 128
DN_HEAD_V = 128
DN_CHUNK = 64
CONV_K = 4
NORM_EPS = 1e-6
L2_EPS = 1e-6
N_MOD = 9

ATTN_WIDTH = ATTN_HEADS * ATTN_HEAD_DIM
ATTN_KV_WIDTH = ATTN_KV_HEADS * ATTN_HEAD_DIM
DN_WIDTH = DN_HEADS * DN_HEAD_V
IN_WIDTH = ATTN_WIDTH + 2 * ATTN_KV_WIDTH + 4 * DN_WIDTH + 2 * DN_HEADS
COL_DN_QKV = 0
COL_DN_GATE = 3 * DN_WIDTH
COL_ATTN_Q = 4 * DN_WIDTH
COL_ATTN_K = COL_ATTN_Q + ATTN_WIDTH
COL_ATTN_V = COL_ATTN_K + ATTN_KV_WIDTH
COL_DN_AB = COL_ATTN_V + ATTN_KV_WIDTH
LANES = 128
SUBLANES = 8
VMEM_LIMIT = 56 * 1024 * 1024
TOKEN_TILE = 512
FFN_TF = 512
MIX_IN_TN = 512
ADALN_TN = 1024


def _cparams(sem):
    return pltpu.CompilerParams(dimension_semantics=sem, vmem_limit_bytes=VMEM_LIMIT)


def _silu(t):
    return t * jax.nn.sigmoid(t)


def _norm_mod(h, nw, shift, scale):
    var = jnp.mean(h * h, axis=-1, keepdims=True)
    y = h * lax.rsqrt(var + NORM_EPS) * nw
    return y * (1.0 + scale) + shift


def _adaln_body(c_ref, w_ref, b_ref, o_ref):
    ca = _silu(c_ref[...]).astype(BF16)
    o_ref[...] = jnp.dot(ca, w_ref[...].astype(BF16), preferred_element_type=F32) + b_ref[...]


def _adaln(c, w, b):
    tn = ADALN_TN
    nb, d = c.shape
    n = w.shape[1]
    rows = -(-nb // SUBLANES) * SUBLANES
    c_pad = jnp.pad(c, ((0, rows - nb), (0, 0)))
    out = pl.pallas_call(
        _adaln_body,
        grid=(n // tn,),
        in_specs=[pl.BlockSpec((rows, d), lambda j: (0, 0)),
                  pl.BlockSpec((d, tn), lambda j: (0, j)),
                  pl.BlockSpec((1, tn), lambda j: (0, j))],
        out_specs=pl.BlockSpec((rows, tn), lambda j: (0, j)),
        out_shape=jax.ShapeDtypeStruct((rows, n), F32),
        compiler_params=_cparams(("arbitrary",)),
        name="adaln",
    )(c_pad, w, b.reshape(1, n))
    return out[:nb]


def _ffn_body(h_ref, nw_ref, sh_ref, sc_ref, g_ref, wg_ref, wu_ref, wd_ref, fn_ref,
              o_ref, u_scr, acc_scr, *, final_norm):
    j = pl.program_id(1)

    @pl.when(j == 0)
    def _():
        u = _norm_mod(h_ref[...], nw_ref[...], sh_ref[0], sc_ref[0])
        u_scr[...] = u.astype(BF16)
        acc_scr[...] = jnp.zeros_like(acc_scr)

    u = u_scr[...]
    gate = jnp.dot(u, wg_ref[...], preferred_element_type=F32)
    up = jnp.dot(u, wu_ref[...], preferred_element_type=F32)
    act = (_silu(gate) * up).astype(BF16)
    acc_scr[...] += jnp.dot(act, wd_ref[...], preferred_element_type=F32)

    @pl.when(j == pl.num_programs(1) - 1)
    def _():
        hn = h_ref[...] + 0.5 * g_ref[0] * acc_scr[...]
        if final_norm:
            var = jnp.mean(hn * hn, axis=-1, keepdims=True)
            hn = hn * lax.rsqrt(var + NORM_EPS) * fn_ref[...]
        o_ref[...] = hn


def _ffn(h, mod3, seq, k0, nw, wg, wu, wd, fn, final_norm):
    tm, tf = TOKEN_TILE, FFN_TF
    t, d = h.shape
    dff = wg.shape[1]
    mod_spec = lambda k: pl.BlockSpec((1, 1, d), lambda i, j: (((i * tm) // seq) * N_MOD + k, 0, 0))
    return pl.pallas_call(
        functools.partial(_ffn_body, final_norm=final_norm),
        grid=(t // tm, dff // tf),
        in_specs=[pl.BlockSpec((tm, d), lambda i, j: (i, 0)),
                  pl.BlockSpec((1, d), lambda i, j: (0, 0)),
                  mod_spec(k0), mod_spec(k0 + 1), mod_spec(k0 + 2),
                  pl.BlockSpec((d, tf), lambda i, j: (0, j)),
                  pl.BlockSpec((d, tf), lambda i, j: (0, j)),
                  pl.BlockSpec((tf, d), lambda i, j: (j, 0)),
                  pl.BlockSpec((1, d), lambda i, j: (0, 0))],
        out_specs=pl.BlockSpec((tm, d), lambda i, j: (i, 0)),
        out_shape=jax.ShapeDtypeStruct((t, d), F32),
        scratch_shapes=[pltpu.VMEM((tm, d), BF16), pltpu.VMEM((tm, d), F32)],
        compiler_params=_cparams(("arbitrary", "arbitrary")),
        name="ffn_final" if final_norm else "ffn",
    )(h, nw.reshape(1, d), mod3, mod3, mod3, wg, wu, wd, fn.reshape(1, d))


def _mix_in_body(h_ref, nw_ref, sh_ref, sc_ref, w_ref, wabt_ref, o_ref, abt_ref, u_scr):
    j = pl.program_id(1)

    @pl.when(j == 0)
    def _():
        u = _norm_mod(h_ref[...], nw_ref[...], sh_ref[0], sc_ref[0]).astype(BF16)
        u_scr[...] = u
        abt_ref[...] = lax.dot_general(wabt_ref[...], u, (((1,), (1,)), ((), ())),
                                       preferred_element_type=F32)

    o_ref[...] = jnp.dot(u_scr[...], w_ref[...], preferred_element_type=F32)


def _mix_in(h, mod3, seq, nw, w_pad, wab_t):
    tm, tn = TOKEN_TILE, MIX_IN_TN
    t, d = h.shape
    n = w_pad.shape[1]
    nab = wab_t.shape[0]
    mod_spec = lambda k: pl.BlockSpec((1, 1, d), lambda i, j: (((i * tm) // seq) * N_MOD + k, 0, 0))
    return pl.pallas_call(
        _mix_in_body,
        grid=(t // tm, n // tn),
        in_specs=[pl.BlockSpec((tm, d), lambda i, j: (i, 0)),
                  pl.BlockSpec((1, d), lambda i, j: (0, 0)),
                  mod_spec(3), mod_spec(4),
                  pl.BlockSpec((d, tn), lambda i, j: (0, j)),
                  pl.BlockSpec((nab, d), lambda i, j: (0, 0))],
        out_specs=[pl.BlockSpec((tm, tn), lambda i, j: (i, j)),
                   pl.BlockSpec((nab, tm), lambda i, j: (0, i))],
        out_shape=[jax.ShapeDtypeStruct((t, n), F32),
                   jax.ShapeDtypeStruct((nab, t), F32)],
        scratch_shapes=[pltpu.VMEM((tm, d), BF16)],
        compiler_params=_cparams(("arbitrary", "arbitrary")),
        name="mix_in",
    )(h, nw.reshape(1, d), mod3, mod3, w_pad, wab_t)


def _rope(t, cos, sin_signed, low_half):
    width = t.shape[-1]
    half = ROPE_DIM // 2
    partner = jnp.where(low_half, pltpu.roll(t, width - half, axis=1), pltpu.roll(t, half, axis=1))
    return t * cos + partner * sin_signed


def _attn_body(sink_ref, pos_ref, invf_ref, q_ref, k_ref, v_ref, o_ref, kprev, vprev):
    n = pl.program_id(1)
    blk = ATTN_BLOCK
    hd = ATTN_HEAD_DIM
    group = ATTN_HEADS // ATTN_KV_HEADS

    @pl.when(n == 0)
    def _():
        kprev[...] = jnp.zeros_like(kprev)
        vprev[...] = jnp.zeros_like(vprev)

    ang = pos_ref[...].astype(F32) * invf_ref[...]
    lane = lax.broadcasted_iota(jnp.int32, (blk, LANES), 1) % hd
    low = lane < (ROPE_DIM // 2)
    cos = jnp.cos(ang)
    sin = jnp.sin(ang)
    sin_signed = jnp.where(low, -sin, sin)

    k = _rope(k_ref[...], cos, sin_signed, low)
    reps = ATTN_WIDTH // LANES
    cos_q = jnp.concatenate([cos] * reps, axis=1)
    sin_q = jnp.concatenate([sin_signed] * reps, axis=1)
    low_q = jnp.concatenate([low] * reps, axis=1)
    q = _rope(q_ref[...], cos_q, sin_q, low_q) * (hd ** -0.5)
    q = q.astype(BF16)

    kband = jnp.concatenate([kprev[...], k.astype(BF16)], axis=0)
    v = v_ref[...].astype(BF16)
    vband = jnp.concatenate([vprev[...], v], axis=0)

    qi = lax.broadcasted_iota(jnp.int32, (blk, 2 * blk), 0)
    kj = lax.broadcasted_iota(jnp.int32, (blk, 2 * blk), 1)
    valid = (kj > qi + (blk - WINDOW)) & (kj <= qi + blk) & ((kj >= blk) | (n > 0))

    for h in range(ATTN_HEADS):
        hk = h // group
        qh = q[:, h * hd:(h + 1) * hd]
        kh = kband[:, hk * hd:(hk + 1) * hd]
        vh = vband[:, hk * hd:(hk + 1) * hd]
        s = lax.dot_general(qh, kh, (((1,), (1,)), ((), ())), preferred_element_type=F32)
        s = jnp.where(valid, s, -jnp.inf)
        sink = sink_ref[h]
        m = jnp.maximum(jnp.max(s, axis=-1, keepdims=True), sink)
        p = jnp.exp(s - m)
        denom = jnp.sum(p, axis=-1, keepdims=True) + jnp.exp(sink - m)
        o = jnp.dot(p.astype(BF16), vh, preferred_element_type=F32) / denom
        o_ref[:, h * hd:(h + 1) * hd] = o.astype(o_ref.dtype)

    kprev[...] = k.astype(BF16)
    vprev[...] = v


def _attention(proj, positions, sinks, batch, seq):
    t = batch * seq
    nb = seq // ATTN_BLOCK
    blk = ATTN_BLOCK
    half = ROPE_DIM // 2
    inv_freq = ROPE_THETA ** (-jnp.arange(0, ROPE_DIM, 2, dtype=F32) / ROPE_DIM)
    head_pat = jnp.concatenate([inv_freq, inv_freq, jnp.zeros((ATTN_HEAD_DIM - 2 * half,), F32)])
    invf = jnp.tile(head_pat, LANES // ATTN_HEAD_DIM).reshape(1, LANES)
    kcol = COL_ATTN_K // LANES
    row = lambda b, n: b * nb + n
    return pl.pallas_call(
        _attn_body,
        grid=(batch, nb),
        in_specs=[pl.BlockSpec(memory_space=pltpu.SMEM),
                  pl.BlockSpec((blk, 1), lambda b, n: (row(b, n), 0)),
                  pl.BlockSpec((1, LANES), lambda b, n: (0, 0)),
                  pl.BlockSpec((blk, ATTN_WIDTH), lambda b, n: (row(b, n), COL_ATTN_Q // ATTN_WIDTH)),
                  pl.BlockSpec((blk, LANES), lambda b, n: (row(b, n), kcol)),
                  pl.BlockSpec((blk, LANES), lambda b, n: (row(b, n), kcol + 1))],
        out_specs=pl.BlockSpec((blk, ATTN_WIDTH), lambda b, n: (row(b, n), 0)),
        out_shape=jax.ShapeDtypeStruct((t, ATTN_WIDTH), BF16),
        scratch_shapes=[pltpu.VMEM((blk, LANES), BF16), pltpu.VMEM((blk, LANES), BF16)],
        compiler_params=_cparams(("arbitrary", "arbitrary")),
        name="attn",
    )(sinks, positions.reshape(t, 1), invf, proj, proj, proj)


INV_BASE = 8


def _split_bf16(t):
    hi = t.astype(BF16)
    lo = (t - hi.astype(F32)).astype(BF16)
    return hi, lo


def _split3_bf16(t):
    hi = t.astype(BF16)
    r1 = t - hi.astype(F32)
    mid = r1.astype(BF16)
    lo = (r1 - mid.astype(F32)).astype(BF16)
    return hi, mid, lo


def _pair_blockdiag(t, left):
    return jnp.concatenate([jnp.where(left, t, 0.0), jnp.where(left, 0.0, t)], axis=0)


def _pair_dot3(a, b, left):
    m = a.shape[0]
    ah, al = _split_bf16(a)
    bh, bl = _split_bf16(_pair_blockdiag(b, left))
    first = jnp.dot(jnp.concatenate([ah, al], axis=0), bh, preferred_element_type=F32)
    return first[:m] + first[m:] + jnp.dot(ah, bl, preferred_element_type=F32)


def _inverse_masks(r, s, c):
    same = lambda bs: (r // bs) == (s // bs)
    masks = [same(INV_BASE)]
    bs = INV_BASE
    while bs < c:
        masks.append(same(2 * bs) & jnp.logical_not(same(bs)))
        bs *= 2
    return masks


def _unit_lower_inverse(a_list, eye, masks, left):
    c = a_list[0].shape[0]
    dot = lambda x, y: _pair_dot3(x, y, left)
    d = [jnp.where(masks[0], a, 0.0) for a in a_list]
    p = [eye - di for di in d]
    x = [dot(di, di) for di in d]
    both = [dot(jnp.concatenate([xi, pi], axis=0), xi) for xi, pi in zip(x, p)]
    p = [pi + bi[c:] for pi, bi in zip(p, both)]
    p = [pi + dot(pi, bi[:c]) for pi, bi in zip(p, both)]
    for m in masks[1:]:
        pm = [dot(pi, jnp.where(m, a, 0.0)) for pi, a in zip(p, a_list)]
        p = [pi - dot(pmi, pi) for pi, pmi in zip(p, pm)]
    return p


def _softplus(t):
    return jnp.maximum(t, 0.0) + jnp.log(1.0 + jnp.exp(-jnp.abs(t)))


def _cumsum_rows(lower, t):
    n = t.shape[1]
    parts = jnp.dot(lower.astype(BF16), jnp.concatenate(_split3_bf16(t), axis=1),
                    preferred_element_type=F32)
    return parts[:, :n] + parts[:, n:2 * n] + parts[:, 2 * n:]


def _cumsum_lanes(t, upper):
    m = t.shape[0]
    parts = jnp.dot(jnp.concatenate(_split3_bf16(t), axis=0), upper.astype(BF16),
                    preferred_element_type=F32)
    return parts[:m] + parts[m:2 * m] + parts[2 * m:]


def _dn_body(qkv_ref, gate_ref, ab_ref, at_ref, cw_ref, alog_ref, dtb_ref,
             alogt_ref, dtbt_ref, nw_ref, o_ref, zbuf, state):
    ci = pl.program_id(1)
    c = DN_CHUNK
    hk = DN_HEAD_K
    dv = DN_HEAD_V
    width = DN_WIDTH
    pad = SUBLANES
    hist = CONV_K - 1
    heads = range(DN_HEADS)
    pairs = range(DN_HEADS // 2)

    @pl.when(ci == 0)
    def _():
        state[...] = jnp.zeros_like(state)
        zbuf[0:pad, :] = jnp.zeros((pad, 3 * width), F32)

    zbuf[pad:pad + c, :] = qkv_ref[...]
    y = jnp.zeros((c, 3 * width), F32)
    for j in range(CONV_K):
        y = y + zbuf[pad - hist + j:pad - hist + j + c, :] * cw_ref[j:j + 1, :]
    zbuf[pad - hist:pad, :] = zbuf[pad + c - hist:pad + c, :]
    y = _silu(y)

    r = lax.broadcasted_iota(jnp.int32, (c, 2 * c), 0)
    lane = lax.broadcasted_iota(jnp.int32, (c, 2 * c), 1)
    left = lane < c
    s = jnp.where(left, lane, lane - c)
    tri = r >= s
    strict = r > s
    eye = jnp.where(r == s, 1.0, 0.0).astype(F32)
    inv_masks = _inverse_masks(r, s, c)

    ab = ab_ref[...]
    g_col_raw = -jnp.exp(alog_ref[...]) * _softplus(ab + dtb_ref[...])
    beta_col = jax.nn.sigmoid(ab)
    rr = lax.broadcasted_iota(jnp.int32, (c, c), 0)
    ss = lax.broadcasted_iota(jnp.int32, (c, c), 1)
    lower = jnp.where(rr >= ss, 1.0, 0.0).astype(F32)
    g_col = _cumsum_rows(lower, g_col_raw)
    g_row_raw = -jnp.exp(alogt_ref[...]) * _softplus(at_ref[0] + dtbt_ref[...])
    r2 = lax.broadcasted_iota(jnp.int32, (2 * c, 2 * c), 0)
    s2 = lax.broadcasted_iota(jnp.int32, (2 * c, 2 * c), 1)
    upper2 = jnp.where((r2 <= s2) & ((r2 < c) == (s2 < c)), 1.0, 0.0).astype(F32)
    g_row = _cumsum_lanes(g_row_raw, upper2)

    q, k, kb, vb, kbg, qg, kd, glast = [], [], [], [], [], [], [], []
    for h in heads:
        qh = y[:, h * hk:(h + 1) * hk]
        kh = y[:, width + h * hk:width + (h + 1) * hk]
        vh = y[:, 2 * width + h * dv:2 * width + (h + 1) * dv]
        qh = qh * lax.rsqrt(jnp.sum(qh * qh, axis=-1, keepdims=True) + L2_EPS) * (hk ** -0.5)
        kh = kh * lax.rsqrt(jnp.sum(kh * kh, axis=-1, keepdims=True) + L2_EPS)
        beta = beta_col[:, DN_HEADS + h:DN_HEADS + h + 1]
        gc = g_col[:, h:h + 1]
        eg = jnp.exp(gc)
        gl = gc[c - 1:c, :]
        q.append(qh)
        k.append(kh)
        kb.append(kh * beta)
        vb.append(vh * beta)
        kbg.append(kb[h] * eg)
        qg.append(qh * eg)
        kd.append(kh * jnp.exp(gl - gc))
        glast.append(gl)

    a_list, ai_list = [], []
    zeros_k = jnp.zeros((c, hk), F32)
    for j in pairs:
        h0, h1 = 2 * j, 2 * j + 1
        gcp = jnp.where(left, jnp.broadcast_to(g_col[:, h0:h0 + 1], (c, 2 * c)),
                        jnp.broadcast_to(g_col[:, h1:h1 + 1], (c, 2 * c)))
        decay = jnp.exp(jnp.where(tri, gcp - g_row[j:j + 1, :], -jnp.inf))
        lhs = jnp.concatenate([jnp.concatenate([kb[h0], kb[h1]], axis=1),
                               jnp.concatenate([q[h0], q[h1]], axis=1)], axis=0)
        kbd = jnp.concatenate([jnp.concatenate([k[h0], zeros_k], axis=1),
                               jnp.concatenate([zeros_k, k[h1]], axis=1)], axis=0)
        kq = lax.dot_general(lhs.astype(BF16), kbd.astype(BF16), (((1,), (1,)), ((), ())),
                             preferred_element_type=F32)
        a_list.append(jnp.where(strict, kq[:c] * decay, 0.0))
        ai_list.append(jnp.where(tri, kq[c:] * decay, 0.0))

    t_list = _unit_lower_inverse(a_list, eye, inv_masks, left)

    u, w = [None] * DN_HEADS, [None] * DN_HEADS
    zeros_v = jnp.zeros((c, dv + hk), F32)
    for j in pairs:
        h0, h1 = 2 * j, 2 * j + 1
        rhs = jnp.concatenate([jnp.concatenate([vb[h0], kbg[h0], zeros_v], axis=1),
                               jnp.concatenate([zeros_v, vb[h1], kbg[h1]], axis=1)], axis=0)
        uw = jnp.dot(t_list[j].astype(BF16), rhs.astype(BF16), preferred_element_type=F32)
        u[h0], w[h0] = uw[:, :dv], uw[:, dv:dv + hk]
        u[h1], w[h1] = uw[:, dv + hk:2 * dv + hk], uw[:, 2 * dv + hk:]

    st = [state[h] for h in heads]
    ws = [jnp.dot(jnp.concatenate([w[h], qg[h]], axis=0).astype(BF16), st[h].astype(BF16),
                  preferred_element_type=F32) for h in heads]
    v_new = [(u[h] - ws[h][:c]).astype(BF16) for h in heads]
    zeros_b = jnp.zeros((c, dv), BF16)
    intra = []
    for j in pairs:
        h0, h1 = 2 * j, 2 * j + 1
        vbd = jnp.concatenate([jnp.concatenate([v_new[h0], zeros_b], axis=1),
                               jnp.concatenate([zeros_b, v_new[h1]], axis=1)], axis=0)
        intra.append(jnp.dot(ai_list[j].astype(BF16), vbd, preferred_element_type=F32))
    for h in heads:
        j, half = divmod(h, 2)
        o = ws[h][c:] + intra[j][:, half * dv:(half + 1) * dv]
        state[h] = st[h] * jnp.exp(glast[h]) + lax.dot_general(
            kd[h].astype(BF16), v_new[h], (((0,), (0,)), ((), ())), preferred_element_type=F32)
        var = jnp.mean(o * o, axis=-1, keepdims=True)
        on = o * lax.rsqrt(var + NORM_EPS) * nw_ref[...]
        hs = slice(h * dv, (h + 1) * dv)
        o_ref[:, hs] = (on * _silu(gate_ref[:, hs])).astype(o_ref.dtype)


def _deltanet(proj, abt, conv_w, a_log, dt_bias, norm_w, batch, seq):
    t = batch * seq
    c = DN_CHUNK
    nc = seq // c
    width = DN_WIDTH
    row = lambda b, i: b * nc + i
    npair = DN_HEADS // 2
    at3 = abt[:DN_HEADS].reshape(npair, 2, t // c, c).transpose(2, 0, 1, 3).reshape(t // c, npair, 2 * c)
    at3 = jnp.pad(at3, ((0, 0), (0, SUBLANES - npair), (0, 0)))
    lane_pad = lambda p: jnp.pad(p.reshape(1, -1), ((0, 0), (0, LANES - p.shape[0])))
    pair_rows = lambda p: jnp.pad(jnp.repeat(p, c).reshape(npair, 2 * c), ((0, SUBLANES - npair), (0, 0)))
    const = lambda shape: pl.BlockSpec(shape, lambda b, i: (0,) * len(shape))
    return pl.pallas_call(
        _dn_body,
        grid=(batch, nc),
        in_specs=[pl.BlockSpec((c, 3 * width), lambda b, i: (row(b, i), COL_DN_QKV // (3 * width))),
                  pl.BlockSpec((c, width), lambda b, i: (row(b, i), COL_DN_GATE // width)),
                  pl.BlockSpec((c, LANES), lambda b, i: (row(b, i), COL_DN_AB // LANES)),
                  pl.BlockSpec((1, SUBLANES, 2 * c), lambda b, i: (row(b, i), 0, 0)),
                  const((CONV_K, 3 * width)),
                  const((1, LANES)), const((1, LANES)),
                  const((SUBLANES, 2 * c)), const((SUBLANES, 2 * c)),
                  const((1, DN_HEAD_V))],
        out_specs=pl.BlockSpec((c, width), lambda b, i: (row(b, i), 0)),
        out_shape=jax.ShapeDtypeStruct((t, width), BF16),
        scratch_shapes=[pltpu.VMEM((SUBLANES + c, 3 * width), F32),
                        pltpu.VMEM((DN_HEADS, DN_HEAD_K, DN_HEAD_V), F32)],
        compiler_params=_cparams(("arbitrary", "arbitrary")),
        name="dn",
    )(proj, proj, proj, at3, conv_w, lane_pad(a_log), lane_pad(dt_bias),
      pair_rows(a_log), pair_rows(dt_bias), norm_w.reshape(1, -1))


def _mix_out_body(h_ref, g_ref, a_ref, d_ref, wa_ref, wd_ref, o_ref):
    mixed = jnp.dot(a_ref[...], wa_ref[...], preferred_element_type=F32)
    mixed = mixed + jnp.dot(d_ref[...], wd_ref[...], preferred_element_type=F32)
    o_ref[...] = h_ref[...] + g_ref[0] * mixed


def _mix_out(h, mod3, seq, attn_out, dn_out, w_out):
    tm = TOKEN_TILE
    t, d = h.shape
    wa = w_out[:ATTN_WIDTH]
    wd = w_out[ATTN_WIDTH:]
    return pl.pallas_call(
        _mix_out_body,
        grid=(t // tm,),
        in_specs=[pl.BlockSpec((tm, d), lambda i: (i, 0)),
                  pl.BlockSpec((1, 1, d), lambda i: (((i * tm) // seq) * N_MOD + 5, 0, 0)),
                  pl.BlockSpec((tm, ATTN_WIDTH), lambda i: (i, 0)),
                  pl.BlockSpec((tm, DN_WIDTH), lambda i: (i, 0)),
                  pl.BlockSpec((ATTN_WIDTH, d), lambda i: (0, 0)),
                  pl.BlockSpec((DN_WIDTH, d), lambda i: (0, 0))],
        out_specs=pl.BlockSpec((tm, d), lambda i: (i, 0)),
        out_shape=jax.ShapeDtypeStruct((t, d), F32),
        compiler_params=_cparams(("arbitrary",)),
        name="mix_out",
    )(h, mod3, attn_out, dn_out, wa, wd)


def kernel(x, c, positions, ada_w, ada_b, norm_ffn1, ffn1_w_gate, ffn1_w_up, ffn1_w_down, norm_mix, w_in, conv_w, a_log, dt_bias, attn_sinks, dn_norm_w, w_out, norm_ffn2, ffn2_w_gate, ffn2_w_up, ffn2_w_down, final_norm):
    batch, seq, d = x.shape
    depth = ada_w.shape[0]
    assert depth >= 1 and seq % ATTN_BLOCK == 0 and seq % DN_CHUNK == 0 and seq % TOKEN_TILE == 0
    t = batch * seq
    h = x.reshape(t, d)
    o_ak, o_av, o_dq = ATTN_WIDTH, ATTN_WIDTH + ATTN_KV_WIDTH, ATTN_WIDTH + 2 * ATTN_KV_WIDTH
    o_dg, o_ab = o_dq + 3 * DN_WIDTH, o_dq + 4 * DN_WIDTH
    n_pad = -(-IN_WIDTH // MIX_IN_TN) * MIX_IN_TN
    for l in range(depth):
        mod3 = _adaln(c, ada_w[l], ada_b[l]).reshape(batch * N_MOD, 1, d)
        h = _ffn(h, mod3, seq, 0, norm_ffn1[l], ffn1_w_gate[l].astype(BF16), ffn1_w_up[l].astype(BF16),
                 ffn1_w_down[l].astype(BF16), final_norm, False)
        w = w_in[l].astype(BF16)
        w_perm = jnp.concatenate([w[:, o_dq:o_ab], w[:, :o_dq], w[:, o_ab:],
                                  jnp.zeros((d, n_pad - IN_WIDTH), BF16)], axis=1)
        proj, abt = _mix_in(h, mod3, seq, norm_mix[l], w_perm, w[:, o_ab:].T)
        attn_out = _attention(proj, positions, attn_sinks[l], batch, seq)
        dn_out = _deltanet(proj, abt, conv_w[l], a_log[l], dt_bias[l], dn_norm_w[l], batch, seq)
        h = _mix_out(h, mod3, seq, attn_out, dn_out, w_out[l].astype(BF16))
        h = _ffn(h, mod3, seq, 6, norm_ffn2[l], ffn2_w_gate[l].astype(BF16), ffn2_w_up[l].astype(BF16),
                 ffn2_w_down[l].astype(BF16), final_norm, l == depth - 1)
    return h.reshape(batch, seq, d)
```

```python
import functools

import numpy as np
import jax
import jax.numpy as jnp
from jax import lax
from jax.experimental import pallas as pl
from jax.experimental.pallas import tpu as pltpu

F32 = jnp.float32
BF16 = jnp.bfloat16

ATTN_HEADS = 16
ATTN_KV_HEADS = 2
ATTN_HEAD_DIM = 64
WINDOW = 128
ATTN_BLOCK = 128
ROPE_THETA = 500000.0
ROPE_DIM = ATTN_HEAD_DIM // 4
DN_HEADS = 8
DN_HEAD_K = 128
DN_HEAD_V = 128
DN_CHUNK = 64
CONV_K = 4
NORM_EPS = 1e-6
L2_EPS = 1e-6
N_MOD = 9

ATTN_WIDTH = ATTN_HEADS * ATTN_HEAD_DIM
ATTN_KV_WIDTH = ATTN_KV_HEADS * ATTN_HEAD_DIM
DN_WIDTH = DN_HEADS * DN_HEAD_V
IN_WIDTH = ATTN_WIDTH + 2 * ATTN_KV_WIDTH + 4 * DN_WIDTH + 2 * DN_HEADS
COL_DN_QKV = 0
COL_DN_GATE = 3 * DN_WIDTH
COL_ATTN_Q = 4 * DN_WIDTH
COL_ATTN_K = COL_ATTN_Q + ATTN_WIDTH
COL_ATTN_V = COL_ATTN_K + ATTN_KV_WIDTH
COL_DN_AB = COL_ATTN_V + ATTN_KV_WIDTH
LANES = 128
SUBLANES = 8
BF16_ROWS = 16
VMEM_LIMIT = 56 * 1024 * 1024
TOKEN_TILE = 512
FFN_TF = 512
MIX_IN_TN = 1792
ADALN_TN = 1024


def _cparams(sem):
    return pltpu.CompilerParams(dimension_semantics=sem, vmem_limit_bytes=VMEM_LIMIT)


def _silu(t):
    return t * jax.nn.sigmoid(t)


def _norm_mod(h, nw, shift, scale):
    var = jnp.mean(h * h, axis=-1, keepdims=True)
    y = h * lax.rsqrt(var + NORM_EPS) * nw
    return y * (1.0 + scale) + shift


def _adaln_body(c_ref, w_ref, b_ref, o_ref):
    ca = _silu(c_ref[...]).astype(BF16)
    o_ref[...] = jnp.dot(ca, w_ref[...].astype(BF16), preferred_element_type=F32) + b_ref[...]


def _adaln(c, w, b):
    tn = ADALN_TN
    nb, d = c.shape
    n = w.shape[1]
    rows = -(-nb // SUBLANES) * SUBLANES
    c_pad = jnp.pad(c, ((0, rows - nb), (0, 0)))
    out = pl.pallas_call(
        _adaln_body,
        grid=(n // tn,),
        in_specs=[pl.BlockSpec((rows, d), lambda j: (0, 0)),
                  pl.BlockSpec((d, tn), lambda j: (0, j)),
                  pl.BlockSpec((1, tn), lambda j: (0, j))],
        out_specs=pl.BlockSpec((rows, tn), lambda j: (0, j)),
        out_shape=jax.ShapeDtypeStruct((rows, n), F32),
        compiler_params=_cparams(("arbitrary",)),
        name="adaln",
    )(c_pad, w, b.reshape(1, n))
    return out[:nb]


def _ffn_body(h_ref, nw_ref, sh_ref, sc_ref, g_ref, wg_ref, wu_ref, wd_ref, fn_ref,
              o_ref, u_scr, acc_scr, *, final_norm):
    j = pl.program_id(1)

    @pl.when(j == 0)
    def _():
        u = _norm_mod(h_ref[...], nw_ref[...], sh_ref[0], sc_ref[0])
        u_scr[...] = u.astype(BF16)
        acc_scr[...] = jnp.zeros_like(acc_scr)

    u = u_scr[...]
    gate = jnp.dot(u, wg_ref[...], preferred_element_type=F32)
    up = jnp.dot(u, wu_ref[...], preferred_element_type=F32)
    act = (_silu(gate) * up).astype(BF16)
    acc_scr[...] += jnp.dot(act, wd_ref[...], preferred_element_type=F32)

    @pl.when(j == pl.num_programs(1) - 1)
    def _():
        hn = h_ref[...] + 0.5 * g_ref[0] * acc_scr[...]
        if final_norm:
            var = jnp.mean(hn * hn, axis=-1, keepdims=True)
            hn = hn * lax.rsqrt(var + NORM_EPS) * fn_ref[...]
        o_ref[...] = hn


def _ffn(h, mod3, seq, k0, nw, wg, wu, wd, fn, final_norm):
    tm, tf = TOKEN_TILE, FFN_TF
    t, d = h.shape
    dff = wg.shape[1]
    mod_spec = lambda k: pl.BlockSpec((1, 1, d), lambda i, j: (((i * tm) // seq) * N_MOD + k, 0, 0))
    return pl.pallas_call(
        functools.partial(_ffn_body, final_norm=final_norm),
        grid=(t // tm, dff // tf),
        in_specs=[pl.BlockSpec((tm, d), lambda i, j: (i, 0)),
                  pl.BlockSpec((1, d), lambda i, j: (0, 0)),
                  mod_spec(k0), mod_spec(k0 + 1), mod_spec(k0 + 2),
                  pl.BlockSpec((d, tf), lambda i, j: (0, j)),
                  pl.BlockSpec((d, tf), lambda i, j: (0, j)),
                  pl.BlockSpec((tf, d), lambda i, j: (j, 0)),
                  pl.BlockSpec((1, d), lambda i, j: (0, 0))],
        out_specs=pl.BlockSpec((tm, d), lambda i, j: (i, 0)),
        out_shape=jax.ShapeDtypeStruct((t, d), F32),
        scratch_shapes=[pltpu.VMEM((tm, d), BF16), pltpu.VMEM((tm, d), F32)],
        compiler_params=_cparams(("arbitrary", "arbitrary")),
        name="ffn_final" if final_norm else "ffn",
    )(h, nw.reshape(1, d), mod3, mod3, mod3, wg, wu, wd, fn.reshape(1, d))


def _mix_in_body(h_ref, nw_ref, sh_ref, sc_ref, w_ref, wab_ref, wabt_ref, o_ref, ab_ref, abt_ref, u_scr):
    j = pl.program_id(1)

    @pl.when(j == 0)
    def _():
        u = _norm_mod(h_ref[...], nw_ref[...], sh_ref[0], sc_ref[0]).astype(BF16)
        u_scr[...] = u
        ab_ref[...] = jnp.dot(u, wab_ref[...], preferred_element_type=F32)
        abt_ref[...] = lax.dot_general(wabt_ref[...], u, (((1,), (1,)), ((), ())),
                                       preferred_element_type=F32)

    o_ref[...] = jnp.dot(u_scr[...], w_ref[...], preferred_element_type=F32).astype(o_ref.dtype)


def _mix_in(h, mod3, seq, nw, w_main, wab, wab_t):
    tm, tn = TOKEN_TILE, MIX_IN_TN
    t, d = h.shape
    n = w_main.shape[1]
    nab = wab_t.shape[0]
    mod_spec = lambda k: pl.BlockSpec((1, 1, d), lambda i, j: (((i * tm) // seq) * N_MOD + k, 0, 0))
    return pl.pallas_call(
        _mix_in_body,
        grid=(t // tm, n // tn),
        in_specs=[pl.BlockSpec((tm, d), lambda i, j: (i, 0)),
                  pl.BlockSpec((1, d), lambda i, j: (0, 0)),
                  mod_spec(3), mod_spec(4),
                  pl.BlockSpec((d, tn), lambda i, j: (0, j)),
                  pl.BlockSpec((d, LANES), lambda i, j: (0, 0)),
                  pl.BlockSpec((nab, d), lambda i, j: (0, 0))],
        out_specs=[pl.BlockSpec((tm, tn), lambda i, j: (i, j)),
                   pl.BlockSpec((tm, LANES), lambda i, j: (i, 0)),
                   pl.BlockSpec((nab, tm), lambda i, j: (0, i))],
        out_shape=[jax.ShapeDtypeStruct((t, n), BF16),
                   jax.ShapeDtypeStruct((t, LANES), F32),
                   jax.ShapeDtypeStruct((nab, t), F32)],
        scratch_shapes=[pltpu.VMEM((tm, d), BF16)],
        compiler_params=_cparams(("arbitrary", "arbitrary")),
        name="mix_in",
    )(h, nw.reshape(1, d), mod3, mod3, w_main, wab, wab_t)


def _attn_body(sink_ref, pos_ref, invf_ref, rot_ref, q_ref, k_ref, v_ref, o_ref, kprev, vprev):
    n = pl.program_id(1)
    blk = ATTN_BLOCK
    hd = ATTN_HEAD_DIM
    pairs_per_kv = ATTN_HEADS // ATTN_KV_HEADS // 2
    n_pairs = ATTN_HEADS // 2

    @pl.when(n == 0)
    def _():
        kprev[...] = jnp.zeros_like(kprev)
        vprev[...] = jnp.zeros_like(vprev)

    ang = pos_ref[...].astype(F32) * invf_ref[...]
    cos = jnp.cos(ang)
    sin = jnp.sin(ang)
    rot = rot_ref[...]

    kin = k_ref[...]
    k = kin.astype(F32) * cos + jnp.dot(kin, rot, preferred_element_type=F32) * sin
    qin = jnp.concatenate([q_ref[:, p * LANES:(p + 1) * LANES] for p in range(n_pairs)], axis=0)
    cos_q = jnp.concatenate([cos] * n_pairs, axis=0)
    sin_q = jnp.concatenate([sin] * n_pairs, axis=0)
    q = (qin.astype(F32) * cos_q + jnp.dot(qin, rot, preferred_element_type=F32) * sin_q) * (hd ** -0.5)
    q = q.astype(BF16)

    lane2 = lax.broadcasted_iota(jnp.int32, (2 * blk, LANES), 1)
    left2 = jnp.where(lane2 < hd, 1.0, 0.0).astype(BF16)
    right2 = jnp.where(lane2 < hd, 0.0, 1.0).astype(BF16)
    v = v_ref[...]
    cur = [k.astype(BF16), pltpu.roll(k, hd, axis=1).astype(BF16)]
    cur_v = [v, pltpu.roll(v.astype(F32), hd, axis=1).astype(BF16)]
    kband = [jnp.concatenate([kprev[i], cur[i]], axis=0) for i in range(2)]
    vband = [jnp.concatenate([vprev[i], cur_v[i]], axis=0) for i in range(2)]
    ones_cols = jnp.concatenate([left2, right2], axis=0)

    qi = lax.broadcasted_iota(jnp.int32, (blk, blk), 0)
    kj = lax.broadcasted_iota(jnp.int32, (blk, blk), 1)
    upper = kj > qi
    left = kj < hd
    prev_bias = jnp.where(n > 0, 0.0, -jnp.inf)

    scores = []
    for g in range(ATTN_KV_HEADS):
        a, b = (0, 1) if g == 0 else (1, 0)
        kbd = jnp.concatenate([kband[a] * left2, kband[b] * right2], axis=0)
        qg = q[g * pairs_per_kv * blk:(g + 1) * pairs_per_kv * blk]
        scores.append(lax.dot_general(qg, kbd, (((1,), (1,)), ((), ())), preferred_element_type=F32))

    probs, mx = [], []
    for p in range(n_pairs):
        g, pp = divmod(p, pairs_per_kv)
        sp = scores[g][pp * blk:(pp + 1) * blk]
        tiles, ms = [], []
        for i in range(2):
            comb = jnp.where(upper, sp[:, 2 * i * blk:(2 * i + 1) * blk] + prev_bias,
                             sp[:, (2 * i + 1) * blk:(2 * i + 2) * blk])
            m = jnp.maximum(jnp.max(comb, axis=-1, keepdims=True), sink_ref[2 * p + i])
            e = jnp.exp(comb - m)
            tiles += [jnp.where(upper, e, 0.0).astype(BF16), jnp.where(upper, 0.0, e).astype(BF16)]
            ms.append(m)
        probs.append(jnp.concatenate(tiles, axis=1))
        mx.append(ms)

    for g in range(ATTN_KV_HEADS):
        a, b = (0, 1) if g == 0 else (1, 0)
        vbd = jnp.concatenate([vband[a] * left2, vband[b] * right2], axis=0)
        vext = jnp.concatenate([vbd, ones_cols], axis=1)
        pg = jnp.concatenate(probs[g * pairs_per_kv:(g + 1) * pairs_per_kv], axis=0)
        res = jnp.dot(pg, vext, preferred_element_type=F32)
        for pp in range(pairs_per_kv):
            p = g * pairs_per_kv + pp
            rp = res[pp * blk:(pp + 1) * blk]
            sink_mass = jnp.where(left, jnp.exp(sink_ref[2 * p] - mx[p][0]),
                                  jnp.exp(sink_ref[2 * p + 1] - mx[p][1]))
            o = rp[:, :LANES] / (rp[:, LANES:] + sink_mass)
            o_ref[:, p * LANES:(p + 1) * LANES] = o.astype(o_ref.dtype)

    for i in range(2):
        kprev[i] = cur[i]
        vprev[i] = cur_v[i]


def _attention(proj, positions, sinks, batch, seq):
    assert WINDOW == ATTN_BLOCK and 2 * ATTN_HEAD_DIM == LANES
    t = batch * seq
    nb = seq // ATTN_BLOCK
    blk = ATTN_BLOCK
    half = ROPE_DIM // 2
    inv_freq = ROPE_THETA ** (-jnp.arange(0, ROPE_DIM, 2, dtype=F32) / ROPE_DIM)
    head_pat = jnp.concatenate([inv_freq, inv_freq, jnp.zeros((ATTN_HEAD_DIM - 2 * half,), F32)])
    invf = jnp.tile(head_pat, LANES // ATTN_HEAD_DIM).reshape(1, LANES)
    rot = np.zeros((LANES, LANES), np.float32)
    for base in range(0, LANES, ATTN_HEAD_DIM):
        for d in range(half):
            rot[base + d + half, base + d] = -1.0
            rot[base + d, base + d + half] = 1.0
    kcol = COL_ATTN_K // LANES
    row = lambda b, n: b * nb + n
    return pl.pallas_call(
        _attn_body,
        grid=(batch, nb),
        in_specs=[pl.BlockSpec(memory_space=pltpu.SMEM),
                  pl.BlockSpec((blk, 1), lambda b, n: (row(b, n), 0)),
                  pl.BlockSpec((1, LANES), lambda b, n: (0, 0)),
                  pl.BlockSpec((LANES, LANES), lambda b, n: (0, 0)),
                  pl.BlockSpec((blk, ATTN_WIDTH), lambda b, n: (row(b, n), COL_ATTN_Q // ATTN_WIDTH)),
                  pl.BlockSpec((blk, LANES), lambda b, n: (row(b, n), kcol)),
                  pl.BlockSpec((blk, LANES), lambda b, n: (row(b, n), kcol + 1))],
        out_specs=pl.BlockSpec((blk, ATTN_WIDTH), lambda b, n: (row(b, n), 0)),
        out_shape=jax.ShapeDtypeStruct((t, ATTN_WIDTH), BF16),
        scratch_shapes=[pltpu.VMEM((2, blk, LANES), BF16), pltpu.VMEM((2, blk, LANES), BF16)],
        compiler_params=_cparams(("arbitrary", "arbitrary")),
        name="attn",
    )(sinks, positions.reshape(t, 1), invf, jnp.asarray(rot, BF16), proj, proj, proj)


INV_BASE = 8
DN_STEP_CHUNKS = 2


def _split_bf16(t):
    hi = t.astype(BF16)
    lo = (t - hi.astype(F32)).astype(BF16)
    return hi, lo


def _split3_bf16(t):
    hi = t.astype(BF16)
    r1 = t - hi.astype(F32)
    mid = r1.astype(BF16)
    lo = (r1 - mid.astype(F32)).astype(BF16)
    return hi, mid, lo


def _pair_blockdiag(t, halves):
    return jnp.concatenate([t * halves[0], t * halves[1]], axis=0)


def _pair_dot(a_terms, b_terms, halves):
    m = a_terms[0].shape[0]
    out = None
    for j, b in enumerate(b_terms):
        lhs = a_terms if j == 0 else a_terms[:1]
        prod = jnp.dot(jnp.concatenate(lhs, axis=0), _pair_blockdiag(b, halves),
                       preferred_element_type=F32)
        for i in range(len(lhs)):
            part = prod[i * m:(i + 1) * m]
            out = part if out is None else out + part
    return out


def _inverse_masks(r, s, c):
    same = lambda bs: (r // bs) == (s // bs)
    masks = [same(INV_BASE)]
    bs = INV_BASE
    while bs < c:
        masks.append(same(2 * bs) & jnp.logical_not(same(bs)))
        bs *= 2
    return [jnp.where(m, 1.0, 0.0).astype(BF16) for m in masks]


def _unit_lower_inverse(a_list, eye, masks, halves):
    c = a_list[0].shape[0]
    dot = lambda x, y: _pair_dot(x, y, halves)
    split = _split_bf16
    d = [a * masks[0] for a in a_list]
    p = [eye - di.astype(F32) for di in d]
    x = [dot((di,), (di,)) for di in d]
    both = [dot(split(jnp.concatenate([xi, pi], axis=0)), split(xi))
            for xi, pi in zip(x, p)]
    p = [pi + bi[c:] for pi, bi in zip(p, both)]
    p = [pi + dot(split(pi), split(bi[:c])) for pi, bi in zip(p, both)]
    for m in masks[1:]:
        ps = [split(pi) for pi in p]
        pm = [dot(psi, (a * m,)) for psi, a in zip(ps, a_list)]
        p = [pi - dot(split(pmi), psi) for pi, pmi, psi in zip(p, pm, ps)]
    return p


def _softplus(t):
    return jnp.maximum(t, 0.0) + jnp.log(1.0 + jnp.exp(-jnp.abs(t)))


def _cumsum_rows(lower, t):
    n = t.shape[1]
    parts = jnp.dot(lower.astype(BF16), jnp.concatenate(_split3_bf16(t), axis=1),
                    preferred_element_type=F32)
    return parts[:, :n] + parts[:, n:2 * n] + parts[:, 2 * n:]


def _cumsum_lanes(t, upper):
    m = t.shape[0]
    parts = jnp.dot(jnp.concatenate(_split3_bf16(t), axis=0), upper.astype(BF16),
                    preferred_element_type=F32)
    return parts[:m] + parts[m:2 * m] + parts[2 * m:]


def _dn_body(qkv_ref, gate_ref, ab_ref, at_ref, shift_ref, cw_ref, alog_ref, dtb_ref,
             alogt_ref, dtbt_ref, nw_ref, o_ref, zbuf, state):
    ci = pl.program_id(1)
    c = DN_CHUNK
    nch = DN_STEP_CHUNKS
    rows = nch * c
    hk = DN_HEAD_K
    dv = DN_HEAD_V
    width = DN_WIDTH
    pad = BF16_ROWS
    hist = CONV_K - 1
    heads = range(DN_HEADS)
    pairs = range(DN_HEADS // 2)
    chunks = range(nch)

    @pl.when(ci == 0)
    def _():
        state[...] = jnp.zeros_like(state)
        zbuf[0:pad, :] = jnp.zeros((pad, 3 * width), BF16)

    zbuf[pad:pad + rows, :] = qkv_ref[...]
    shifted = jnp.dot(shift_ref[...], zbuf[...], preferred_element_type=F32)
    y = qkv_ref[...].astype(F32) * cw_ref[hist:hist + 1, :]
    for j in range(hist):
        y = y + shifted[j * rows:(j + 1) * rows] * cw_ref[j:j + 1, :]
    zbuf[0:pad, :] = zbuf[rows:rows + pad, :]
    y = _silu(y)

    n_qk = 2 * DN_HEADS
    sq = y[:, :2 * width] * y[:, :2 * width]
    sq = jnp.concatenate([sq[:, i * hk:(i + 1) * hk] for i in range(n_qk)], axis=0)
    sums = jnp.dot(jnp.concatenate(_split_bf16(sq), axis=0), jnp.ones((hk, hk), BF16),
                   preferred_element_type=F32)
    rnorm = lax.rsqrt(sums[:n_qk * rows] + sums[n_qk * rows:] + L2_EPS)

    r = lax.broadcasted_iota(jnp.int32, (c, 2 * c), 0)
    lane = lax.broadcasted_iota(jnp.int32, (c, 2 * c), 1)
    left = lane < c
    halves = (jnp.where(left, 1.0, 0.0).astype(BF16), jnp.where(left, 0.0, 1.0).astype(BF16))
    s = jnp.where(left, lane, lane - c)
    tri = r >= s
    strict = r > s
    eye = jnp.where(r == s, 1.0, 0.0).astype(F32)
    inv_masks = _inverse_masks(r, s, c)

    ab = ab_ref[...]
    g_col_raw = -jnp.exp(alog_ref[...]) * _softplus(ab + dtb_ref[...])
    beta_col = jax.nn.sigmoid(ab)
    rr = lax.broadcasted_iota(jnp.int32, (rows, rows), 0)
    ss = lax.broadcasted_iota(jnp.int32, (rows, rows), 1)
    lower = jnp.where((rr >= ss) & ((rr // c) == (ss // c)), 1.0, 0.0).astype(F32)
    g_col = _cumsum_rows(lower, g_col_raw)
    g_row_raw = jnp.concatenate(
        [-jnp.exp(alogt_ref[...]) * _softplus(at_ref[cc] + dtbt_ref[...]) for cc in chunks], axis=0)
    r2 = lax.broadcasted_iota(jnp.int32, (2 * c, 2 * c), 0)
    s2 = lax.broadcasted_iota(jnp.int32, (2 * c, 2 * c), 1)
    upper2 = jnp.where((r2 <= s2) & ((r2 < c) == (s2 < c)), 1.0, 0.0).astype(F32)
    g_row = _cumsum_lanes(g_row_raw, upper2)

    q, k, kb, vb, kbg, qg = [], [], [], [], [], []
    for h in heads:
        qh = y[:, h * hk:(h + 1) * hk]
        kh = y[:, width + h * hk:width + (h + 1) * hk]
        vh = y[:, 2 * width + h * dv:2 * width + (h + 1) * dv]
        qh = qh * rnorm[h * rows:(h + 1) * rows] * (hk ** -0.5)
        kh = kh * rnorm[(DN_HEADS + h) * rows:(DN_HEADS + h + 1) * rows]
        beta = beta_col[:, DN_HEADS + h:DN_HEADS + h + 1]
        eg = jnp.exp(g_col[:, h:h + 1])
        q.append(qh)
        k.append(kh)
        kb.append(kh * beta)
        vb.append(vh * beta)
        kbg.append(kb[h] * eg)
        qg.append(qh * eg)

    a_list, ai_list = [], []
    zeros_k = jnp.zeros((c, hk), F32)
    for cc in chunks:
        cs = slice(cc * c, (cc + 1) * c)
        for j in pairs:
            h0, h1 = 2 * j, 2 * j + 1
            gcp = jnp.where(left, jnp.broadcast_to(g_col[cs, h0:h0 + 1], (c, 2 * c)),
                            jnp.broadcast_to(g_col[cs, h1:h1 + 1], (c, 2 * c)))
            grp = g_row[cc * SUBLANES + j:cc * SUBLANES + j + 1, :]
            decay = jnp.exp(jnp.where(tri, gcp - grp, -jnp.inf))
            lhs = jnp.concatenate([jnp.concatenate([kb[h0][cs], kb[h1][cs]], axis=1),
                                   jnp.concatenate([q[h0][cs], q[h1][cs]], axis=1)], axis=0)
            kbd = jnp.concatenate([jnp.concatenate([k[h0][cs], zeros_k], axis=1),
                                   jnp.concatenate([zeros_k, k[h1][cs]], axis=1)], axis=0)
            kq = lax.dot_general(lhs.astype(BF16), kbd.astype(BF16), (((1,), (1,)), ((), ())),
                                 preferred_element_type=F32)
            a_list.append(jnp.where(strict, kq[:c] * decay, 0.0).astype(BF16))
            ai_list.append(jnp.where(tri, kq[c:] * decay, 0.0).astype(BF16))

    t_list = _unit_lower_inverse(a_list, eye, inv_masks, halves)

    u = [[None] * DN_HEADS for _ in chunks]
    w = [[None] * DN_HEADS for _ in chunks]
    zeros_v = jnp.zeros((c, dv + hk), F32)
    for cc in chunks:
        cs = slice(cc * c, (cc + 1) * c)
        for j in pairs:
            h0, h1 = 2 * j, 2 * j + 1
            rhs = jnp.concatenate([jnp.concatenate([vb[h0][cs], kbg[h0][cs], zeros_v], axis=1),
                                   jnp.concatenate([zeros_v, vb[h1][cs], kbg[h1][cs]], axis=1)], axis=0)
            uw = jnp.dot(t_list[cc * len(pairs) + j].astype(BF16), rhs.astype(BF16),
                         preferred_element_type=F32)
            u[cc][h0], w[cc][h0] = uw[:, :dv], uw[:, dv:dv + hk]
            u[cc][h1], w[cc][h1] = uw[:, dv + hk:2 * dv + hk], uw[:, 2 * dv + hk:]

    st = [state[h] for h in heads]
    zeros_b = jnp.zeros((c, dv), BF16)
    for cc in chunks:
        cs = slice(cc * c, (cc + 1) * c)
        ws = [jnp.dot(jnp.concatenate([w[cc][h], qg[h][cs]], axis=0).astype(BF16), st[h].astype(BF16),
                      preferred_element_type=F32) for h in heads]
        v_new = [(u[cc][h] - ws[h][:c]).astype(BF16) for h in heads]
        intra = []
        for j in pairs:
            h0, h1 = 2 * j, 2 * j + 1
            vbd = jnp.concatenate([jnp.concatenate([v_new[h0], zeros_b], axis=1),
                                   jnp.concatenate([zeros_b, v_new[h1]], axis=1)], axis=0)
            intra.append(jnp.dot(ai_list[cc * len(pairs) + j], vbd, preferred_element_type=F32))
        for h in heads:
            j, half = divmod(h, 2)
            o = ws[h][c:] + intra[j][:, half * dv:(half + 1) * dv]
            gc = g_col[cs, h:h + 1]
            gl = gc[c - 1:c, :]
            kd = k[h][cs] * jnp.exp(gl - gc)
            st[h] = st[h] * jnp.exp(gl) + lax.dot_general(
                kd.astype(BF16), v_new[h], (((0,), (0,)), ((), ())), preferred_element_type=F32)
            var = jnp.mean(o * o, axis=-1, keepdims=True)
            on = o * lax.rsqrt(var + NORM_EPS) * nw_ref[...]
            hs = slice(h * dv, (h + 1) * dv)
            o_ref[cs, hs] = (on * _silu(gate_ref[cs, hs].astype(F32))).astype(o_ref.dtype)
    for h in heads:
        state[h] = st[h]


def _deltanet(proj, ab, abt, conv_w, a_log, dt_bias, norm_w, batch, seq):
    t = batch * seq
    c = DN_CHUNK
    nch = DN_STEP_CHUNKS
    rows = nch * c
    ns = seq // rows
    width = DN_WIDTH
    row = lambda b, i: b * ns + i
    npair = DN_HEADS // 2
    at3 = abt[:DN_HEADS].reshape(npair, 2, t // c, c).transpose(2, 0, 1, 3).reshape(t // c, npair, 2 * c)
    at3 = jnp.pad(at3, ((0, 0), (0, SUBLANES - npair), (0, 0)))
    lane_pad = lambda p: jnp.pad(p.reshape(1, -1), ((0, 0), (0, LANES - p.shape[0])))
    pair_rows = lambda p: jnp.pad(jnp.repeat(p, c).reshape(npair, 2 * c), ((0, SUBLANES - npair), (0, 0)))
    const = lambda shape: pl.BlockSpec(shape, lambda b, i: (0,) * len(shape))
    hist = CONV_K - 1
    shift = np.zeros((hist * rows, BF16_ROWS + rows), np.float32)
    for j in range(hist):
        shift[j * rows + np.arange(rows), BF16_ROWS + np.arange(rows) - hist + j] = 1.0
    return pl.pallas_call(
        _dn_body,
        grid=(batch, ns),
        in_specs=[pl.BlockSpec((rows, 3 * width), lambda b, i: (row(b, i), COL_DN_QKV // (3 * width))),
                  pl.BlockSpec((rows, width), lambda b, i: (row(b, i), COL_DN_GATE // width)),
                  pl.BlockSpec((rows, LANES), lambda b, i: (row(b, i), 0)),
                  pl.BlockSpec((nch, SUBLANES, 2 * c), lambda b, i: (row(b, i), 0, 0)),
                  const((hist * rows, BF16_ROWS + rows)),
                  const((CONV_K, 3 * width)),
                  const((1, LANES)), const((1, LANES)),
                  const((SUBLANES, 2 * c)), const((SUBLANES, 2 * c)),
                  const((1, DN_HEAD_V))],
        out_specs=pl.BlockSpec((rows, width), lambda b, i: (row(b, i), 0)),
        out_shape=jax.ShapeDtypeStruct((t, width), BF16),
        scratch_shapes=[pltpu.VMEM((BF16_ROWS + rows, 3 * width), BF16),
                        pltpu.VMEM((DN_HEADS, DN_HEAD_K, DN_HEAD_V), F32)],
        compiler_params=_cparams(("arbitrary", "arbitrary")),
        name="dn",
    )(proj, proj, ab, at3, jnp.asarray(shift, BF16), conv_w, lane_pad(a_log), lane_pad(dt_bias),
      pair_rows(a_log), pair_rows(dt_bias), norm_w.reshape(1, -1))


def _mix_out_body(h_ref, g_ref, a_ref, d_ref, wa_ref, wd_ref, o_ref):
    mixed = jnp.dot(a_ref[...], wa_ref[...], preferred_element_type=F32)
    mixed = mixed + jnp.dot(d_ref[...], wd_ref[...], preferred_element_type=F32)
    o_ref[...] = h_ref[...] + g_ref[0] * mixed


def _mix_out(h, mod3, seq, attn_out, dn_out, w_out):
    tm = TOKEN_TILE
    t, d = h.shape
    wa = w_out[:ATTN_WIDTH]
    wd = w_out[ATTN_WIDTH:]
    return pl.pallas_call(
        _mix_out_body,
        grid=(t // tm,),
        in_specs=[pl.BlockSpec((tm, d), lambda i: (i, 0)),
                  pl.BlockSpec((1, 1, d), lambda i: (((i * tm) // seq) * N_MOD + 5, 0, 0)),
                  pl.BlockSpec((tm, ATTN_WIDTH), lambda i: (i, 0)),
                  pl.BlockSpec((tm, DN_WIDTH), lambda i: (i, 0)),
                  pl.BlockSpec((ATTN_WIDTH, d), lambda i: (0, 0)),
                  pl.BlockSpec((DN_WIDTH, d), lambda i: (0, 0))],
        out_specs=pl.BlockSpec((tm, d), lambda i: (i, 0)),
        out_shape=jax.ShapeDtypeStruct((t, d), F32),
        compiler_params=_cparams(("arbitrary",)),
        name="mix_out",
    )(h, mod3, attn_out, dn_out, wa, wd)


def kernel(x, c, positions, ada_w, ada_b, norm_ffn1, ffn1_w_gate, ffn1_w_up, ffn1_w_down, norm_mix, w_in, conv_w, a_log, dt_bias, attn_sinks, dn_norm_w, w_out, norm_ffn2, ffn2_w_gate, ffn2_w_up, ffn2_w_down, final_norm):
    batch, seq, d = x.shape
    depth = ada_w.shape[0]
    assert depth >= 1 and seq % ATTN_BLOCK == 0 and seq % TOKEN_TILE == 0
    assert seq % (DN_CHUNK * DN_STEP_CHUNKS) == 0
    t = batch * seq
    h = x.reshape(t, d)
    o_ak, o_av, o_dq = ATTN_WIDTH, ATTN_WIDTH + ATTN_KV_WIDTH, ATTN_WIDTH + 2 * ATTN_KV_WIDTH
    o_dg, o_ab = o_dq + 3 * DN_WIDTH, o_dq + 4 * DN_WIDTH
    assert COL_DN_AB % MIX_IN_TN == 0
    for l in range(depth):
        mod3 = _adaln(c, ada_w[l], ada_b[l]).reshape(batch * N_MOD, 1, d)
        h = _ffn(h, mod3, seq, 0, norm_ffn1[l], ffn1_w_gate[l].astype(BF16), ffn1_w_up[l].astype(BF16),
                 ffn1_w_down[l].astype(BF16), final_norm, False)
        w = w_in[l].astype(BF16)
        w_main = jnp.concatenate([w[:, o_dq:o_ab], w[:, :o_dq]], axis=1)
        w_ab = jnp.pad(w[:, o_ab:], ((0, 0), (0, LANES - 2 * DN_HEADS)))
        proj, ab, abt = _mix_in(h, mod3, seq, norm_mix[l], w_main, w_ab, w[:, o_ab:].T)
        attn_out = _attention(proj, positions, attn_sinks[l], batch, seq)
        dn_out = _deltanet(proj, ab, abt, conv_w[l], a_log[l], dt_bias[l], dn_norm_w[l], batch, seq)
        h = _mix_out(h, mod3, seq, attn_out, dn_out, w_out[l].astype(BF16))
        h = _ffn(h, mod3, seq, 6, norm_ffn2[l], ffn2_w_gate[l].astype(BF16), ffn2_w_up[l].astype(BF16),
                 ffn2_w_down[l].astype(BF16), final_norm, l == depth - 1)
    return h.reshape(batch, seq, d)
```

```python
import functools

import numpy as np
import jax
import jax.numpy as jnp
from jax import lax
from jax.experimental import pallas as pl
from jax.experimental.pallas import tpu as pltpu

F32 = jnp.float32
BF16 = jnp.bfloat16

ATTN_HEADS = 16
ATTN_KV_HEADS = 2
ATTN_HEAD_DIM = 64
WINDOW = 128
ATTN_BLOCK = 128
ROPE_THETA = 500000.0
ROPE_DIM = ATTN_HEAD_DIM // 4
DN_HEADS = 8
DN_HEAD_K = 128
DN_HEAD_V = 128
DN_CHUNK = 64
CONV_K = 4
NORM_EPS = 1e-6
L2_EPS = 1e-6
N_MOD = 9

ATTN_WIDTH = ATTN_HEADS * ATTN_HEAD_DIM
ATTN_KV_WIDTH = ATTN_KV_HEADS * ATTN_HEAD_DIM
DN_WIDTH = DN_HEADS * DN_HEAD_V
IN_WIDTH = ATTN_WIDTH + 2 * ATTN_KV_WIDTH + 4 * DN_WIDTH + 2 * DN_HEADS
COL_DN_QKV = 0
COL_DN_GATE = 3 * DN_WIDTH
COL_ATTN_Q = 4 * DN_WIDTH
COL_ATTN_K = COL_ATTN_Q + ATTN_WIDTH
COL_ATTN_V = COL_ATTN_K + ATTN_KV_WIDTH
COL_DN_AB = COL_ATTN_V + ATTN_KV_WIDTH
LANES = 128
SUBLANES = 8
BF16_ROWS = 16
VMEM_LIMIT = 56 * 1024 * 1024
FFN_VMEM_LIMIT = 60 * 1024 * 1024
TOKEN_TILE = 512
FFN_TM = 1024
FFN_TF = 256
MIX_IN_TN = 1792
ADALN_TN = 1024


def _cparams(sem, vmem_limit=VMEM_LIMIT):
    return pltpu.CompilerParams(dimension_semantics=sem, vmem_limit_bytes=vmem_limit)


def _silu(t):
    return t * jax.nn.sigmoid(t)


def _norm_mod(h, nw, shift, scale):
    var = jnp.mean(h * h, axis=-1, keepdims=True)
    y = h * lax.rsqrt(var + NORM_EPS) * nw
    return y * (1.0 + scale) + shift


def _adaln_body(c_ref, w_ref, b_ref, o_ref):
    ca = _silu(c_ref[...]).astype(BF16)
    o_ref[...] = jnp.dot(ca, w_ref[...].astype(BF16), preferred_element_type=F32) + b_ref[...]


def _adaln(c, w, b):
    tn = ADALN_TN
    nb, d = c.shape
    n = w.shape[1]
    rows = -(-nb // SUBLANES) * SUBLANES
    c_pad = jnp.pad(c, ((0, rows - nb), (0, 0)))
    out = pl.pallas_call(
        _adaln_body,
        grid=(n // tn,),
        in_specs=[pl.BlockSpec((rows, d), lambda j: (0, 0)),
                  pl.BlockSpec((d, tn), lambda j: (0, j)),
                  pl.BlockSpec((1, tn), lambda j: (0, j))],
        out_specs=pl.BlockSpec((rows, tn), lambda j: (0, j)),
        out_shape=jax.ShapeDtypeStruct((rows, n), F32),
        compiler_params=_cparams(("arbitrary",)),
        name="adaln",
    )(c_pad, w, b.reshape(1, n))
    return out[:nb]


def _ffn_body(h_ref, nw_ref, sh_ref, sc_ref, g_ref, wg_ref, wu_ref, wd_ref, fn_ref,
              o_ref, u_scr, *, final_norm):
    j = pl.program_id(1)

    @pl.when(j == 0)
    def _():
        u = _norm_mod(h_ref[...], nw_ref[...], sh_ref[0], sc_ref[0])
        u_scr[...] = u.astype(BF16)
        o_ref[...] = jnp.zeros_like(o_ref)

    u = u_scr[...]
    gate = jnp.dot(u, wg_ref[...].astype(BF16), preferred_element_type=F32)
    up = jnp.dot(u, wu_ref[...].astype(BF16), preferred_element_type=F32)
    act = (_silu(gate) * up).astype(BF16)
    o_ref[...] += jnp.dot(act, wd_ref[...].astype(BF16), preferred_element_type=F32)

    @pl.when(j == pl.num_programs(1) - 1)
    def _():
        hn = h_ref[...] + 0.5 * g_ref[0] * o_ref[...]
        if final_norm:
            var = jnp.mean(hn * hn, axis=-1, keepdims=True)
            hn = hn * lax.rsqrt(var + NORM_EPS) * fn_ref[...]
        o_ref[...] = hn


def _ffn(h, mod3, seq, k0, nw, wg, wu, wd, fn, final_norm):
    tm, tf = FFN_TM, FFN_TF
    t, d = h.shape
    dff = wg.shape[1]
    mod_spec = lambda k: pl.BlockSpec((1, 1, d), lambda i, j: (((i * tm) // seq) * N_MOD + k, 0, 0))
    return pl.pallas_call(
        functools.partial(_ffn_body, final_norm=final_norm),
        grid=(t // tm, dff // tf),
        in_specs=[pl.BlockSpec((tm, d), lambda i, j: (i, 0)),
                  pl.BlockSpec((1, d), lambda i, j: (0, 0)),
                  mod_spec(k0), mod_spec(k0 + 1), mod_spec(k0 + 2),
                  pl.BlockSpec((d, tf), lambda i, j: (0, j)),
                  pl.BlockSpec((d, tf), lambda i, j: (0, j)),
                  pl.BlockSpec((tf, d), lambda i, j: (j, 0)),
                  pl.BlockSpec((1, d), lambda i, j: (0, 0))],
        out_specs=pl.BlockSpec((tm, d), lambda i, j: (i, 0)),
        out_shape=jax.ShapeDtypeStruct((t, d), F32),
        scratch_shapes=[pltpu.VMEM((tm, d), BF16)],
        compiler_params=_cparams(("arbitrary", "arbitrary"), FFN_VMEM_LIMIT),
        name="ffn_final" if final_norm else "ffn",
    )(h, nw.reshape(1, d), mod3, mod3, mod3, wg, wu, wd, fn.reshape(1, d))


def _mix_in_body(h_ref, nw_ref, sh_ref, sc_ref, w_ref, wab_ref, o_ref, ab_ref, u_scr):
    j = pl.program_id(1)

    @pl.when(j == 0)
    def _():
        u = _norm_mod(h_ref[...], nw_ref[...], sh_ref[0], sc_ref[0]).astype(BF16)
        u_scr[...] = u
        ab_ref[...] = jnp.dot(u, wab_ref[...], preferred_element_type=F32)

    o_ref[...] = jnp.dot(u_scr[...], w_ref[...], preferred_element_type=F32).astype(o_ref.dtype)


def _mix_in(h, mod3, seq, nw, w_main, wab):
    tm, tn = TOKEN_TILE, MIX_IN_TN
    t, d = h.shape
    n = w_main.shape[1]
    mod_spec = lambda k: pl.BlockSpec((1, 1, d), lambda i, j: (((i * tm) // seq) * N_MOD + k, 0, 0))
    return pl.pallas_call(
        _mix_in_body,
        grid=(t // tm, n // tn),
        in_specs=[pl.BlockSpec((tm, d), lambda i, j: (i, 0)),
                  pl.BlockSpec((1, d), lambda i, j: (0, 0)),
                  mod_spec(3), mod_spec(4),
                  pl.BlockSpec((d, tn), lambda i, j: (0, j)),
                  pl.BlockSpec((d, LANES), lambda i, j: (0, 0))],
        out_specs=[pl.BlockSpec((tm, tn), lambda i, j: (i, j)),
                   pl.BlockSpec((tm, LANES), lambda i, j: (i, 0))],
        out_shape=[jax.ShapeDtypeStruct((t, n), BF16),
                   jax.ShapeDtypeStruct((t, LANES), F32)],
        scratch_shapes=[pltpu.VMEM((tm, d), BF16)],
        compiler_params=_cparams(("arbitrary", "arbitrary")),
        name="mix_in",
    )(h, nw.reshape(1, d), mod3, mod3, w_main, wab)


def _attn_body(sink_ref, pos_ref, invf_ref, rot_ref, q_ref, k_ref, v_ref, o_ref, kprev, vprev):
    n = pl.program_id(1)
    blk = ATTN_BLOCK
    hd = ATTN_HEAD_DIM
    pairs_per_kv = ATTN_HEADS // ATTN_KV_HEADS // 2
    n_pairs = ATTN_HEADS // 2

    @pl.when(n == 0)
    def _():
        kprev[...] = jnp.zeros_like(kprev)
        vprev[...] = jnp.zeros_like(vprev)

    ang = pos_ref[...].astype(F32) * invf_ref[...]
    cos = jnp.cos(ang)
    sin = jnp.sin(ang)
    rot = rot_ref[...]

    kin = k_ref[...]
    k = kin.astype(F32) * cos + jnp.dot(kin, rot, preferred_element_type=F32) * sin
    qin = jnp.concatenate([q_ref[:, p * LANES:(p + 1) * LANES] for p in range(n_pairs)], axis=0)
    cos_q = jnp.concatenate([cos] * n_pairs, axis=0)
    sin_q = jnp.concatenate([sin] * n_pairs, axis=0)
    q = (qin.astype(F32) * cos_q + jnp.dot(qin, rot, preferred_element_type=F32) * sin_q) * (hd ** -0.5)
    q = q.astype(BF16)

    lane2 = lax.broadcasted_iota(jnp.int32, (2 * blk, LANES), 1)
    left2 = jnp.where(lane2 < hd, 1.0, 0.0).astype(BF16)
    right2 = jnp.where(lane2 < hd, 0.0, 1.0).astype(BF16)
    v = v_ref[...]
    cur = [k.astype(BF16), pltpu.roll(k, hd, axis=1).astype(BF16)]
    cur_v = [v, pltpu.roll(v.astype(F32), hd, axis=1).astype(BF16)]
    kband = [jnp.concatenate([kprev[i], cur[i]], axis=0) for i in range(2)]
    vband = [jnp.concatenate([vprev[i], cur_v[i]], axis=0) for i in range(2)]
    ones_cols = jnp.concatenate([left2, right2], axis=0)

    qi = lax.broadcasted_iota(jnp.int32, (blk, blk), 0)
    kj = lax.broadcasted_iota(jnp.int32, (blk, blk), 1)
    upper = kj > qi
    left = kj < hd
    prev_bias = jnp.where(n > 0, 0.0, -jnp.inf)

    scores = []
    for g in range(ATTN_KV_HEADS):
        a, b = (0, 1) if g == 0 else (1, 0)
        kbd = jnp.concatenate([kband[a] * left2, kband[b] * right2], axis=0)
        qg = q[g * pairs_per_kv * blk:(g + 1) * pairs_per_kv * blk]
        scores.append(lax.dot_general(qg, kbd, (((1,), (1,)), ((), ())), preferred_element_type=F32))

    probs, mx = [], []
    for p in range(n_pairs):
        g, pp = divmod(p, pairs_per_kv)
        sp = scores[g][pp * blk:(pp + 1) * blk]
        tiles, ms = [], []
        for i in range(2):
            comb = jnp.where(upper, sp[:, 2 * i * blk:(2 * i + 1) * blk] + prev_bias,
                             sp[:, (2 * i + 1) * blk:(2 * i + 2) * blk])
            m = jnp.maximum(jnp.max(comb, axis=-1, keepdims=True), sink_ref[2 * p + i])
            e = jnp.exp(comb - m)
            tiles += [jnp.where(upper, e, 0.0).astype(BF16), jnp.where(upper, 0.0, e).astype(BF16)]
            ms.append(m)
        probs.append(jnp.concatenate(tiles, axis=1))
        mx.append(ms)

    for g in range(ATTN_KV_HEADS):
        a, b = (0, 1) if g == 0 else (1, 0)
        vbd = jnp.concatenate([vband[a] * left2, vband[b] * right2], axis=0)
        vext = jnp.concatenate([vbd, ones_cols], axis=1)
        pg = jnp.concatenate(probs[g * pairs_per_kv:(g + 1) * pairs_per_kv], axis=0)
        res = jnp.dot(pg, vext, preferred_element_type=F32)
        for pp in range(pairs_per_kv):
            p = g * pairs_per_kv + pp
            rp = res[pp * blk:(pp + 1) * blk]
            sink_mass = jnp.where(left, jnp.exp(sink_ref[2 * p] - mx[p][0]),
                                  jnp.exp(sink_ref[2 * p + 1] - mx[p][1]))
            o = rp[:, :LANES] / (rp[:, LANES:] + sink_mass)
            o_ref[:, p * LANES:(p + 1) * LANES] = o.astype(o_ref.dtype)

    for i in range(2):
        kprev[i] = cur[i]
        vprev[i] = cur_v[i]


def _attention(proj, positions, sinks, batch, seq):
    assert WINDOW == ATTN_BLOCK and 2 * ATTN_HEAD_DIM == LANES
    t = batch * seq
    nb = seq // ATTN_BLOCK
    blk = ATTN_BLOCK
    half = ROPE_DIM // 2
    inv_freq = ROPE_THETA ** (-jnp.arange(0, ROPE_DIM, 2, dtype=F32) / ROPE_DIM)
    head_pat = jnp.concatenate([inv_freq, inv_freq, jnp.zeros((ATTN_HEAD_DIM - 2 * half,), F32)])
    invf = jnp.tile(head_pat, LANES // ATTN_HEAD_DIM).reshape(1, LANES)
    rot = np.zeros((LANES, LANES), np.float32)
    for base in range(0, LANES, ATTN_HEAD_DIM):
        for d in range(half):
            rot[base + d + half, base + d] = -1.0
            rot[base + d, base + d + half] = 1.0
    kcol = COL_ATTN_K // LANES
    row = lambda b, n: b * nb + n
    return pl.pallas_call(
        _attn_body,
        grid=(batch, nb),
        in_specs=[pl.BlockSpec(memory_space=pltpu.SMEM),
                  pl.BlockSpec((blk, 1), lambda b, n: (row(b, n), 0)),
                  pl.BlockSpec((1, LANES), lambda b, n: (0, 0)),
                  pl.BlockSpec((LANES, LANES), lambda b, n: (0, 0)),
                  pl.BlockSpec((blk, ATTN_WIDTH), lambda b, n: (row(b, n), COL_ATTN_Q // ATTN_WIDTH)),
                  pl.BlockSpec((blk, LANES), lambda b, n: (row(b, n), kcol)),
                  pl.BlockSpec((blk, LANES), lambda b, n: (row(b, n), kcol + 1))],
        out_specs=pl.BlockSpec((blk, ATTN_WIDTH), lambda b, n: (row(b, n), 0)),
        out_shape=jax.ShapeDtypeStruct((t, ATTN_WIDTH), BF16),
        scratch_shapes=[pltpu.VMEM((2, blk, LANES), BF16), pltpu.VMEM((2, blk, LANES), BF16)],
        compiler_params=_cparams(("arbitrary", "arbitrary")),
        name="attn",
    )(sinks, positions.reshape(t, 1), invf, jnp.asarray(rot, BF16), proj, proj, proj)


INV_BASE = 8
DN_STEP_CHUNKS = 2


def _split_bf16(t):
    hi = t.astype(BF16)
    lo = (t - hi.astype(F32)).astype(BF16)
    return hi, lo


def _split3_bf16(t):
    hi = t.astype(BF16)
    r1 = t - hi.astype(F32)
    mid = r1.astype(BF16)
    lo = (r1 - mid.astype(F32)).astype(BF16)
    return hi, mid, lo


def _pair_blockdiag(t, halves):
    return jnp.concatenate([t * halves[0], t * halves[1]], axis=0)


def _pair_dot(a_terms, b_terms, halves):
    m = a_terms[0].shape[0]
    out = None
    for j, b in enumerate(b_terms):
        lhs = a_terms if j == 0 else a_terms[:1]
        prod = jnp.dot(jnp.concatenate(lhs, axis=0), _pair_blockdiag(b, halves),
                       preferred_element_type=F32)
        for i in range(len(lhs)):
            part = prod[i * m:(i + 1) * m]
            out = part if out is None else out + part
    return out


def _inverse_masks(r, s, c):
    same = lambda bs: (r // bs) == (s // bs)
    masks = [same(INV_BASE)]
    bs = INV_BASE
    while bs < c:
        masks.append(same(2 * bs) & jnp.logical_not(same(bs)))
        bs *= 2
    return [jnp.where(m, 1.0, 0.0).astype(BF16) for m in masks]


def _unit_lower_inverse(a_list, eye, masks, halves):
    c = a_list[0].shape[0]
    dot = lambda x, y: _pair_dot(x, y, halves)
    split = _split_bf16
    d = [a * masks[0] for a in a_list]
    p = [eye - di.astype(F32) for di in d]
    x = [dot((di,), (di,)) for di in d]
    both = [dot(split(jnp.concatenate([xi, pi], axis=0)), split(xi))
            for xi, pi in zip(x, p)]
    p = [pi + bi[c:] for pi, bi in zip(p, both)]
    p = [pi + dot(split(pi), split(bi[:c])) for pi, bi in zip(p, both)]
    for m in masks[1:]:
        ps = [split(pi) for pi in p]
        pm = [dot(psi, (a * m,)) for psi, a in zip(ps, a_list)]
        p = [pi - dot(split(pmi), psi) for pi, pmi, psi in zip(p, pm, ps)]
    return p


def _softplus(t):
    return jnp.maximum(t, 0.0) + jnp.log(1.0 + jnp.exp(-jnp.abs(t)))


def _cumsum_rows(lower, t):
    n = t.shape[1]
    parts = jnp.dot(lower.astype(BF16), jnp.concatenate(_split3_bf16(t), axis=1),
                    preferred_element_type=F32)
    return parts[:, :n] + parts[:, n:2 * n] + parts[:, 2 * n:]


def _cumsum_lanes(t, upper):
    m = t.shape[0]
    parts = jnp.dot(jnp.concatenate(_split3_bf16(t), axis=0), upper.astype(BF16),
                    preferred_element_type=F32)
    return parts[:m] + parts[m:2 * m] + parts[2 * m:]


def _dn_body(qkv_ref, gate_ref, ab_ref, shift_ref, cw_ref, alog_ref, dtb_ref,
             alogt_ref, dtbt_ref, nw_ref, o_ref, zbuf, state):
    ci = pl.program_id(1)
    c = DN_CHUNK
    nch = DN_STEP_CHUNKS
    rows = nch * c
    hk = DN_HEAD_K
    dv = DN_HEAD_V
    width = DN_WIDTH
    pad = BF16_ROWS
    hist = CONV_K - 1
    heads = range(DN_HEADS)
    pairs = range(DN_HEADS // 2)
    chunks = range(nch)

    @pl.when(ci == 0)
    def _():
        state[...] = jnp.zeros_like(state)
        zbuf[0:pad, :] = jnp.zeros((pad, 3 * width), BF16)

    zbuf[pad:pad + rows, :] = qkv_ref[...]
    shifted = jnp.dot(shift_ref[...], zbuf[...], preferred_element_type=F32)
    y = qkv_ref[...].astype(F32) * cw_ref[hist:hist + 1, :]
    for j in range(hist):
        y = y + shifted[j * rows:(j + 1) * rows] * cw_ref[j:j + 1, :]
    zbuf[0:pad, :] = zbuf[rows:rows + pad, :]
    y = _silu(y)

    n_qk = 2 * DN_HEADS
    sq = y[:, :2 * width] * y[:, :2 * width]
    sq = jnp.concatenate([sq[:, i * hk:(i + 1) * hk] for i in range(n_qk)], axis=0)
    sums = jnp.dot(jnp.concatenate(_split_bf16(sq), axis=0), jnp.ones((hk, hk), BF16),
                   preferred_element_type=F32)
    rnorm = lax.rsqrt(sums[:n_qk * rows] + sums[n_qk * rows:] + L2_EPS)

    r = lax.broadcasted_iota(jnp.int32, (c, 2 * c), 0)
    lane = lax.broadcasted_iota(jnp.int32, (c, 2 * c), 1)
    left = lane < c
    halves = (jnp.where(left, 1.0, 0.0).astype(BF16), jnp.where(left, 0.0, 1.0).astype(BF16))
    s = jnp.where(left, lane, lane - c)
    tri = r >= s
    strict = r > s
    eye = jnp.where(r == s, 1.0, 0.0).astype(F32)
    inv_masks = _inverse_masks(r, s, c)

    ab = ab_ref[...]
    g_col_raw = -jnp.exp(alog_ref[...]) * _softplus(ab + dtb_ref[...])
    beta_col = jax.nn.sigmoid(ab)
    rr = lax.broadcasted_iota(jnp.int32, (rows, rows), 0)
    ss = lax.broadcasted_iota(jnp.int32, (rows, rows), 1)
    lower = jnp.where((rr >= ss) & ((rr // c) == (ss // c)), 1.0, 0.0).astype(F32)
    g_col = _cumsum_rows(lower, g_col_raw)
    a_rows = ab.T[0:DN_HEADS]
    g_row_raw = -jnp.exp(alogt_ref[...]) * _softplus(a_rows + dtbt_ref[...])
    upper = jnp.where((rr <= ss) & ((rr // c) == (ss // c)), 1.0, 0.0).astype(F32)
    g_row = _cumsum_lanes(g_row_raw, upper)
    g_row_sw = pltpu.roll(g_row, c, axis=1)

    q, k, kb, vb, kbg, qg = [], [], [], [], [], []
    for h in heads:
        qh = y[:, h * hk:(h + 1) * hk]
        kh = y[:, width + h * hk:width + (h + 1) * hk]
        vh = y[:, 2 * width + h * dv:2 * width + (h + 1) * dv]
        qh = qh * rnorm[h * rows:(h + 1) * rows] * (hk ** -0.5)
        kh = kh * rnorm[(DN_HEADS + h) * rows:(DN_HEADS + h + 1) * rows]
        beta = beta_col[:, DN_HEADS + h:DN_HEADS + h + 1]
        eg = jnp.exp(g_col[:, h:h + 1])
        q.append(qh)
        k.append(kh)
        kb.append(kh * beta)
        vb.append(vh * beta)
        kbg.append(kb[h] * eg)
        qg.append(qh * eg)

    a_list, ai_list = [], []
    zeros_k = jnp.zeros((c, hk), F32)
    for cc in chunks:
        cs = slice(cc * c, (cc + 1) * c)
        for j in pairs:
            h0, h1 = 2 * j, 2 * j + 1
            gcp = jnp.where(left, jnp.broadcast_to(g_col[cs, h0:h0 + 1], (c, 2 * c)),
                            jnp.broadcast_to(g_col[cs, h1:h1 + 1], (c, 2 * c)))
            grp = (jnp.where(left[0:1], g_row[h0:h0 + 1], g_row_sw[h1:h1 + 1]) if cc == 0 else
                   jnp.where(left[0:1], g_row_sw[h0:h0 + 1], g_row[h1:h1 + 1]))
            decay = jnp.exp(jnp.where(tri, gcp - grp, -jnp.inf))
            lhs = jnp.concatenate([jnp.concatenate([kb[h0][cs], kb[h1][cs]], axis=1),
                                   jnp.concatenate([q[h0][cs], q[h1][cs]], axis=1)], axis=0)
            kbd = jnp.concatenate([jnp.concatenate([k[h0][cs], zeros_k], axis=1),
                                   jnp.concatenate([zeros_k, k[h1][cs]], axis=1)], axis=0)
            kq = lax.dot_general(lhs.astype(BF16), kbd.astype(BF16), (((1,), (1,)), ((), ())),
                                 preferred_element_type=F32)
            a_list.append(jnp.where(strict, kq[:c] * decay, 0.0).astype(BF16))
            ai_list.append(jnp.where(tri, kq[c:] * decay, 0.0).astype(BF16))

    t_list = _unit_lower_inverse(a_list, eye, inv_masks, halves)

    u = [[None] * DN_HEADS for _ in chunks]
    w = [[None] * DN_HEADS for _ in chunks]
    zeros_v = jnp.zeros((c, dv + hk), F32)
    for cc in chunks:
        cs = slice(cc * c, (cc + 1) * c)
        for j in pairs:
            h0, h1 = 2 * j, 2 * j + 1
            rhs = jnp.concatenate([jnp.concatenate([vb[h0][cs], kbg[h0][cs], zeros_v], axis=1),
                                   jnp.concatenate([zeros_v, vb[h1][cs], kbg[h1][cs]], axis=1)], axis=0)
            uw = jnp.dot(t_list[cc * len(pairs) + j].astype(BF16), rhs.astype(BF16),
                         preferred_element_type=F32)
            u[cc][h0], w[cc][h0] = uw[:, :dv], uw[:, dv:dv + hk]
            u[cc][h1], w[cc][h1] = uw[:, dv + hk:2 * dv + hk], uw[:, 2 * dv + hk:]

    st = [state[h] for h in heads]
    zeros_b = jnp.zeros((c, dv), BF16)
    for cc in chunks:
        cs = slice(cc * c, (cc + 1) * c)
        ws = [jnp.dot(jnp.concatenate([w[cc][h], qg[h][cs]], axis=0).astype(BF16), st[h].astype(BF16),
                      preferred_element_type=F32) for h in heads]
        v_new = [(u[cc][h] - ws[h][:c]).astype(BF16) for h in heads]
        intra = []
        for j in pairs:
            h0, h1 = 2 * j, 2 * j + 1
            vbd = jnp.concatenate([jnp.concatenate([v_new[h0], zeros_b], axis=1),
                                   jnp.concatenate([zeros_b, v_new[h1]], axis=1)], axis=0)
            intra.append(jnp.dot(ai_list[cc * len(pairs) + j], vbd, preferred_element_type=F32))
        for h in heads:
            j, half = divmod(h, 2)
            o = ws[h][c:] + intra[j][:, half * dv:(half + 1) * dv]
            gc = g_col[cs, h:h + 1]
            gl = gc[c - 1:c, :]
            kd = k[h][cs] * jnp.exp(gl - gc)
            st[h] = st[h] * jnp.exp(gl) + lax.dot_general(
                kd.astype(BF16), v_new[h], (((0,), (0,)), ((), ())), preferred_element_type=F32)
            var = jnp.mean(o * o, axis=-1, keepdims=True)
            on = o * lax.rsqrt(var + NORM_EPS) * nw_ref[...]
            hs = slice(h * dv, (h + 1) * dv)
            o_ref[cs, hs] = (on * _silu(gate_ref[cs, hs].astype(F32))).astype(o_ref.dtype)
    for h in heads:
        state[h] = st[h]


def _deltanet(proj, ab, conv_w, a_log, dt_bias, norm_w, batch, seq):
    t = batch * seq
    c = DN_CHUNK
    nch = DN_STEP_CHUNKS
    rows = nch * c
    ns = seq // rows
    width = DN_WIDTH
    row = lambda b, i: b * ns + i
    assert rows == LANES
    lane_pad = lambda p: jnp.pad(p.reshape(1, -1), ((0, 0), (0, LANES - p.shape[0])))
    const = lambda shape: pl.BlockSpec(shape, lambda b, i: (0,) * len(shape))
    hist = CONV_K - 1
    shift = np.zeros((hist * rows, BF16_ROWS + rows), np.float32)
    for j in range(hist):
        shift[j * rows + np.arange(rows), BF16_ROWS + np.arange(rows) - hist + j] = 1.0
    return pl.pallas_call(
        _dn_body,
        grid=(batch, ns),
        in_specs=[pl.BlockSpec((rows, 3 * width), lambda b, i: (row(b, i), COL_DN_QKV // (3 * width))),
                  pl.BlockSpec((rows, width), lambda b, i: (row(b, i), COL_DN_GATE // width)),
                  pl.BlockSpec((rows, LANES), lambda b, i: (row(b, i), 0)),
                  const((hist * rows, BF16_ROWS + rows)),
                  const((CONV_K, 3 * width)),
                  const((1, LANES)), const((1, LANES)),
                  const((DN_HEADS, 1)), const((DN_HEADS, 1)),
                  const((1, DN_HEAD_V))],
        out_specs=pl.BlockSpec((rows, width), lambda b, i: (row(b, i), 0)),
        out_shape=jax.ShapeDtypeStruct((t, width), BF16),
        scratch_shapes=[pltpu.VMEM((BF16_ROWS + rows, 3 * width), BF16),
                        pltpu.VMEM((DN_HEADS, DN_HEAD_K, DN_HEAD_V), F32)],
        compiler_params=_cparams(("arbitrary", "arbitrary")),
        name="dn",
    )(proj, proj, ab, jnp.asarray(shift, BF16), conv_w, lane_pad(a_log), lane_pad(dt_bias),
      a_log.reshape(-1, 1), dt_bias.reshape(-1, 1), norm_w.reshape(1, -1))


def _mix_out_body(h_ref, g_ref, a_ref, d_ref, wa_ref, wd_ref, o_ref):
    mixed = jnp.dot(a_ref[...], wa_ref[...], preferred_element_type=F32)
    mixed = mixed + jnp.dot(d_ref[...], wd_ref[...], preferred_element_type=F32)
    o_ref[...] = h_ref[...] + g_ref[0] * mixed


def _mix_out(h, mod3, seq, attn_out, dn_out, w_out):
    tm = TOKEN_TILE
    t, d = h.shape
    wa = w_out[:ATTN_WIDTH]
    wd = w_out[ATTN_WIDTH:]
    return pl.pallas_call(
        _mix_out_body,
        grid=(t // tm,),
        in_specs=[pl.BlockSpec((tm, d), lambda i: (i, 0)),
                  pl.BlockSpec((1, 1, d), lambda i: (((i * tm) // seq) * N_MOD + 5, 0, 0)),
                  pl.BlockSpec((tm, ATTN_WIDTH), lambda i: (i, 0)),
                  pl.BlockSpec((tm, DN_WIDTH), lambda i: (i, 0)),
                  pl.BlockSpec((ATTN_WIDTH, d), lambda i: (0, 0)),
                  pl.BlockSpec((DN_WIDTH, d), lambda i: (0, 0))],
        out_specs=pl.BlockSpec((tm, d), lambda i: (i, 0)),
        out_shape=jax.ShapeDtypeStruct((t, d), F32),
        compiler_params=_cparams(("arbitrary",)),
        name="mix_out",
    )(h, mod3, attn_out, dn_out, wa, wd)


def kernel(x, c, positions, ada_w, ada_b, norm_ffn1, ffn1_w_gate, ffn1_w_up, ffn1_w_down, norm_mix, w_in, conv_w, a_log, dt_bias, attn_sinks, dn_norm_w, w_out, norm_ffn2, ffn2_w_gate, ffn2_w_up, ffn2_w_down, final_norm):
    batch, seq, d = x.shape
    depth = ada_w.shape[0]
    assert depth >= 1 and seq % ATTN_BLOCK == 0 and seq % TOKEN_TILE == 0
    assert seq % (DN_CHUNK * DN_STEP_CHUNKS) == 0
    t = batch * seq
    h = x.reshape(t, d)
    o_ak, o_av, o_dq = ATTN_WIDTH, ATTN_WIDTH + ATTN_KV_WIDTH, ATTN_WIDTH + 2 * ATTN_KV_WIDTH
    o_dg, o_ab = o_dq + 3 * DN_WIDTH, o_dq + 4 * DN_WIDTH
    assert COL_DN_AB % MIX_IN_TN == 0
    for l in range(depth):
        mod3 = _adaln(c, ada_w[l], ada_b[l]).reshape(batch * N_MOD, 1, d)
        h = _ffn(h, mod3, seq, 0, norm_ffn1[l], ffn1_w_gate[l], ffn1_w_up[l], ffn1_w_down[l],
                 final_norm, False)
        w = w_in[l].astype(BF16)
        w_main = jnp.concatenate([w[:, o_dq:o_ab], w[:, :o_dq]], axis=1)
        w_ab = jnp.pad(w[:, o_ab:], ((0, 0), (0, LANES - 2 * DN_HEADS)))
        proj, ab = _mix_in(h, mod3, seq, norm_mix[l], w_main, w_ab)
        attn_out = _attention(proj, positions, attn_sinks[l], batch, seq)
        dn_out = _deltanet(proj, ab, conv_w[l], a_log[l], dt_bias[l], dn_norm_w[l], batch, seq)
        h = _mix_out(h, mod3, seq, attn_out, dn_out, w_out[l].astype(BF16))
        h = _ffn(h, mod3, seq, 6, norm_ffn2[l], ffn2_w_gate[l], ffn2_w_up[l], ffn2_w_down[l],
                 final_norm, l == depth - 1)
    return h.reshape(batch, seq, d)
```

```python
import functools

import numpy as np
import jax
import jax.numpy as jnp
from jax import lax
from jax.experimental import pallas as pl
from jax.experimental.pallas import tpu as pltpu

F32 = jnp.float32
BF16 = jnp.bfloat16

ATTN_HEADS = 16
ATTN_KV_HEADS = 2
ATTN_HEAD_DIM = 64
WINDOW = 128
ATTN_BLOCK = 128
ROPE_THETA = 500000.0
ROPE_DIM = ATTN_HEAD_DIM // 4
DN_HEADS = 8
DN_HEAD_K = 128
DN_HEAD_V = 128
DN_CHUNK = 64
CONV_K = 4
NORM_EPS = 1e-6
L2_EPS = 1e-6
N_MOD = 9

ATTN_WIDTH = ATTN_HEADS * ATTN_HEAD_DIM
ATTN_KV_WIDTH = ATTN_KV_HEADS * ATTN_HEAD_DIM
DN_WIDTH = DN_HEADS * DN_HEAD_V
IN_WIDTH = ATTN_WIDTH + 2 * ATTN_KV_WIDTH + 4 * DN_WIDTH + 2 * DN_HEADS
COL_DN_QKV = 0
COL_DN_GATE = 3 * DN_WIDTH
COL_ATTN_Q = 4 * DN_WIDTH
COL_ATTN_K = COL_ATTN_Q + ATTN_WIDTH
COL_ATTN_V = COL_ATTN_K + ATTN_KV_WIDTH
COL_DN_AB = COL_ATTN_V + ATTN_KV_WIDTH
LANES = 128
SUBLANES = 8
BF16_ROWS = 16
VMEM_LIMIT = 56 * 1024 * 1024
FFN_VMEM_LIMIT = 60 * 1024 * 1024
TOKEN_TILE = 512
FFN_TM = 1024
FFN_TF = 256
MIX_IN_TM = 1024
MIX_IN_TN = 1792
ADALN_TN = 1024
NORM_ROWS = 128


def _cparams(sem, vmem_limit=VMEM_LIMIT):
    return pltpu.CompilerParams(dimension_semantics=sem, vmem_limit_bytes=vmem_limit)


def _silu(t):
    return t * jax.nn.sigmoid(t)


def _norm_mod_to(u_ref, h_ref, nw, shift, scale):
    gain = nw * (1.0 + scale)
    for r in range(0, h_ref.shape[0], NORM_ROWS):
        h = h_ref[r:r + NORM_ROWS, :]
        var = jnp.mean(h * h, axis=-1, keepdims=True)
        u_ref[r:r + NORM_ROWS, :] = (h * lax.rsqrt(var + NORM_EPS) * gain + shift).astype(u_ref.dtype)


def _adaln_body(c_ref, w_ref, b_ref, o_ref):
    ca = _silu(c_ref[...]).astype(BF16)
    o_ref[...] = jnp.dot(ca, w_ref[...].astype(BF16), preferred_element_type=F32) + b_ref[...]


def _adaln(c, w, b):
    tn = ADALN_TN
    nb, d = c.shape
    n = w.shape[1]
    rows = -(-nb // SUBLANES) * SUBLANES
    c_pad = jnp.pad(c, ((0, rows - nb), (0, 0)))
    out = pl.pallas_call(
        _adaln_body,
        grid=(n // tn,),
        in_specs=[pl.BlockSpec((rows, d), lambda j: (0, 0)),
                  pl.BlockSpec((d, tn), lambda j: (0, j)),
                  pl.BlockSpec((1, tn), lambda j: (0, j))],
        out_specs=pl.BlockSpec((rows, tn), lambda j: (0, j)),
        out_shape=jax.ShapeDtypeStruct((rows, n), F32),
        compiler_params=_cparams(("arbitrary",)),
        name="adaln",
    )(c_pad, w, b.reshape(1, n))
    return out[:nb]


def _ffn_body(h_ref, nw_ref, sh_ref, sc_ref, g_ref, wg_ref, wu_ref, wd_ref, fn_ref,
              o_ref, u_scr, *, final_norm):
    j = pl.program_id(1)

    @pl.when(j == 0)
    def _():
        _norm_mod_to(u_scr, h_ref, nw_ref[...], sh_ref[0], sc_ref[0])
        o_ref[...] = jnp.zeros_like(o_ref)

    u = u_scr[...]
    gate = jnp.dot(u, wg_ref[...].astype(BF16), preferred_element_type=F32)
    up = jnp.dot(u, wu_ref[...].astype(BF16), preferred_element_type=F32)
    act = (_silu(gate) * up).astype(BF16)
    o_ref[...] += jnp.dot(act, wd_ref[...].astype(BF16), preferred_element_type=F32)

    @pl.when(j == pl.num_programs(1) - 1)
    def _():
        hn = h_ref[...] + 0.5 * g_ref[0] * o_ref[...]
        if final_norm:
            var = jnp.mean(hn * hn, axis=-1, keepdims=True)
            hn = hn * lax.rsqrt(var + NORM_EPS) * fn_ref[...]
        o_ref[...] = hn


def _ffn(h, mod3, seq, k0, nw, wg, wu, wd, fn, final_norm):
    tm, tf = FFN_TM, FFN_TF
    t, d = h.shape
    dff = wg.shape[1]
    mod_spec = lambda k: pl.BlockSpec((1, 1, d), lambda i, j: (((i * tm) // seq) * N_MOD + k, 0, 0))
    return pl.pallas_call(
        functools.partial(_ffn_body, final_norm=final_norm),
        grid=(t // tm, dff // tf),
        in_specs=[pl.BlockSpec((tm, d), lambda i, j: (i, 0)),
                  pl.BlockSpec((1, d), lambda i, j: (0, 0)),
                  mod_spec(k0), mod_spec(k0 + 1), mod_spec(k0 + 2),
                  pl.BlockSpec((d, tf), lambda i, j: (0, j)),
                  pl.BlockSpec((d, tf), lambda i, j: (0, j)),
                  pl.BlockSpec((tf, d), lambda i, j: (j, 0)),
                  pl.BlockSpec((1, d), lambda i, j: (0, 0))],
        out_specs=pl.BlockSpec((tm, d), lambda i, j: (i, 0)),
        out_shape=jax.ShapeDtypeStruct((t, d), F32),
        scratch_shapes=[pltpu.VMEM((tm, d), BF16)],
        compiler_params=_cparams(("arbitrary", "arbitrary"), FFN_VMEM_LIMIT),
        name="ffn_final" if final_norm else "ffn",
    )(h, nw.reshape(1, d), mod3, mod3, mod3, wg, wu, wd, fn.reshape(1, d))


def _mix_in_body(h_ref, nw_ref, sh_ref, sc_ref, w_ref, wab_ref, o_ref, ab_ref, u_scr):
    j = pl.program_id(1)

    @pl.when(j == 0)
    def _():
        _norm_mod_to(u_scr, h_ref, nw_ref[...], sh_ref[0], sc_ref[0])
        ab_ref[...] = jnp.dot(u_scr[...], wab_ref[...], preferred_element_type=F32)

    o_ref[...] = jnp.dot(u_scr[...], w_ref[...], preferred_element_type=F32).astype(o_ref.dtype)


def _mix_in(h, mod3, seq, nw, w_main, wab):
    tm, tn = MIX_IN_TM, MIX_IN_TN
    t, d = h.shape
    n = w_main.shape[1]
    mod_spec = lambda k: pl.BlockSpec((1, 1, d), lambda i, j: (((i * tm) // seq) * N_MOD + k, 0, 0))
    return pl.pallas_call(
        _mix_in_body,
        grid=(t // tm, n // tn),
        in_specs=[pl.BlockSpec((tm, d), lambda i, j: (i, 0)),
                  pl.BlockSpec((1, d), lambda i, j: (0, 0)),
                  mod_spec(3), mod_spec(4),
                  pl.BlockSpec((d, tn), lambda i, j: (0, j)),
                  pl.BlockSpec((d, LANES), lambda i, j: (0, 0))],
        out_specs=[pl.BlockSpec((tm, tn), lambda i, j: (i, j)),
                   pl.BlockSpec((tm, LANES), lambda i, j: (i, 0))],
        out_shape=[jax.ShapeDtypeStruct((t, n), BF16),
                   jax.ShapeDtypeStruct((t, LANES), F32)],
        scratch_shapes=[pltpu.VMEM((tm, d), BF16)],
        compiler_params=_cparams(("arbitrary", "arbitrary")),
        name="mix_in",
    )(h, nw.reshape(1, d), mod3, mod3, w_main, wab)


def _attn_body(sink_ref, pos_ref, invf_ref, rot_ref, q_ref, k_ref, v_ref, o_ref, kprev, vprev):
    n = pl.program_id(1)
    blk = ATTN_BLOCK
    hd = ATTN_HEAD_DIM
    pairs_per_kv = ATTN_HEADS // ATTN_KV_HEADS // 2
    n_pairs = ATTN_HEADS // 2

    @pl.when(n == 0)
    def _():
        kprev[...] = jnp.zeros_like(kprev)
        vprev[...] = jnp.zeros_like(vprev)

    ang = pos_ref[...].astype(F32) * invf_ref[...]
    cos = jnp.cos(ang)
    sin = jnp.sin(ang)
    rot = rot_ref[...]

    kin = k_ref[...]
    k = kin.astype(F32) * cos + jnp.dot(kin, rot, preferred_element_type=F32) * sin
    qin = jnp.concatenate([q_ref[:, p * LANES:(p + 1) * LANES] for p in range(n_pairs)], axis=0)
    cos_q = jnp.concatenate([cos] * n_pairs, axis=0)
    sin_q = jnp.concatenate([sin] * n_pairs, axis=0)
    q = (qin.astype(F32) * cos_q + jnp.dot(qin, rot, preferred_element_type=F32) * sin_q) * (hd ** -0.5)
    q = q.astype(BF16)

    lane2 = lax.broadcasted_iota(jnp.int32, (2 * blk, LANES), 1)
    left2 = jnp.where(lane2 < hd, 1.0, 0.0).astype(BF16)
    right2 = jnp.where(lane2 < hd, 0.0, 1.0).astype(BF16)
    v = v_ref[...]
    cur = [k.astype(BF16), pltpu.roll(k, hd, axis=1).astype(BF16)]
    cur_v = [v, pltpu.roll(v.astype(F32), hd, axis=1).astype(BF16)]
    kband = [jnp.concatenate([kprev[i], cur[i]], axis=0) for i in range(2)]
    vband = [jnp.concatenate([vprev[i], cur_v[i]], axis=0) for i in range(2)]
    ones_cols = jnp.concatenate([left2, right2], axis=0)

    qi = lax.broadcasted_iota(jnp.int32, (blk, blk), 0)
    kj = lax.broadcasted_iota(jnp.int32, (blk, blk), 1)
    upper = kj > qi
    left = kj < hd
    prev_bias = jnp.where(n > 0, 0.0, -jnp.inf)

    scores = []
    for g in range(ATTN_KV_HEADS):
        a, b = (0, 1) if g == 0 else (1, 0)
        kbd = jnp.concatenate([kband[a] * left2, kband[b] * right2], axis=0)
        qg = q[g * pairs_per_kv * blk:(g + 1) * pairs_per_kv * blk]
        scores.append(lax.dot_general(qg, kbd, (((1,), (1,)), ((), ())), preferred_element_type=F32))

    probs, mx = [], []
    for p in range(n_pairs):
        g, pp = divmod(p, pairs_per_kv)
        sp = scores[g][pp * blk:(pp + 1) * blk]
        tiles, ms = [], []
        for i in range(2):
            comb = jnp.where(upper, sp[:, 2 * i * blk:(2 * i + 1) * blk] + prev_bias,
                             sp[:, (2 * i + 1) * blk:(2 * i + 2) * blk])
            m = jnp.maximum(jnp.max(comb, axis=-1, keepdims=True), sink_ref[2 * p + i])
            e = jnp.exp(comb - m)
            tiles += [jnp.where(upper, e, 0.0).astype(BF16), jnp.where(upper, 0.0, e).astype(BF16)]
            ms.append(m)
        probs.append(jnp.concatenate(tiles, axis=1))
        mx.append(ms)

    for g in range(ATTN_KV_HEADS):
        a, b = (0, 1) if g == 0 else (1, 0)
        vbd = jnp.concatenate([vband[a] * left2, vband[b] * right2], axis=0)
        vext = jnp.concatenate([vbd, ones_cols], axis=1)
        pg = jnp.concatenate(probs[g * pairs_per_kv:(g + 1) * pairs_per_kv], axis=0)
        res = jnp.dot(pg, vext, preferred_element_type=F32)
        for pp in range(pairs_per_kv):
            p = g * pairs_per_kv + pp
            rp = res[pp * blk:(pp + 1) * blk]
            sink_mass = jnp.where(left, jnp.exp(sink_ref[2 * p] - mx[p][0]),
                                  jnp.exp(sink_ref[2 * p + 1] - mx[p][1]))
            o = rp[:, :LANES] / (rp[:, LANES:] + sink_mass)
            o_ref[:, p * LANES:(p + 1) * LANES] = o.astype(o_ref.dtype)

    for i in range(2):
        kprev[i] = cur[i]
        vprev[i] = cur_v[i]


def _attention(proj, positions, sinks, batch, seq):
    assert WINDOW == ATTN_BLOCK and 2 * ATTN_HEAD_DIM == LANES
    t = batch * seq
    nb = seq // ATTN_BLOCK
    blk = ATTN_BLOCK
    half = ROPE_DIM // 2
    inv_freq = ROPE_THETA ** (-jnp.arange(0, ROPE_DIM, 2, dtype=F32) / ROPE_DIM)
    head_pat = jnp.concatenate([inv_freq, inv_freq, jnp.zeros((ATTN_HEAD_DIM - 2 * half,), F32)])
    invf = jnp.tile(head_pat, LANES // ATTN_HEAD_DIM).reshape(1, LANES)
    rot = np.zeros((LANES, LANES), np.float32)
    for base in range(0, LANES, ATTN_HEAD_DIM):
        for d in range(half):
            rot[base + d + half, base + d] = -1.0
            rot[base + d, base + d + half] = 1.0
    kcol = COL_ATTN_K // LANES
    row = lambda b, n: b * nb + n
    return pl.pallas_call(
        _attn_body,
        grid=(batch, nb),
        in_specs=[pl.BlockSpec(memory_space=pltpu.SMEM),
                  pl.BlockSpec((blk, 1), lambda b, n: (row(b, n), 0)),
                  pl.BlockSpec((1, LANES), lambda b, n: (0, 0)),
                  pl.BlockSpec((LANES, LANES), lambda b, n: (0, 0)),
                  pl.BlockSpec((blk, ATTN_WIDTH), lambda b, n: (row(b, n), COL_ATTN_Q // ATTN_WIDTH)),
                  pl.BlockSpec((blk, LANES), lambda b, n: (row(b, n), kcol)),
                  pl.BlockSpec((blk, LANES), lambda b, n: (row(b, n), kcol + 1))],
        out_specs=pl.BlockSpec((blk, ATTN_WIDTH), lambda b, n: (row(b, n), 0)),
        out_shape=jax.ShapeDtypeStruct((t, ATTN_WIDTH), BF16),
        scratch_shapes=[pltpu.VMEM((2, blk, LANES), BF16), pltpu.VMEM((2, blk, LANES), BF16)],
        compiler_params=_cparams(("arbitrary", "arbitrary")),
        name="attn",
    )(sinks, positions.reshape(t, 1), invf, jnp.asarray(rot, BF16), proj, proj, proj)


INV_BASE = 8
DN_STEP_CHUNKS = 2


def _split_bf16(t):
    hi = t.astype(BF16)
    lo = (t - hi.astype(F32)).astype(BF16)
    return hi, lo


def _split3_bf16(t):
    hi = t.astype(BF16)
    r1 = t - hi.astype(F32)
    mid = r1.astype(BF16)
    lo = (r1 - mid.astype(F32)).astype(BF16)
    return hi, mid, lo


def _pair_blockdiag(t, halves):
    return jnp.concatenate([t * halves[0], t * halves[1]], axis=0)


def _pair_dot(a_terms, b_terms, halves):
    m = a_terms[0].shape[0]
    out = None
    for j, b in enumerate(b_terms):
        lhs = a_terms if j == 0 else a_terms[:1]
        prod = jnp.dot(jnp.concatenate(lhs, axis=0), _pair_blockdiag(b, halves),
                       preferred_element_type=F32)
        for i in range(len(lhs)):
            part = prod[i * m:(i + 1) * m]
            out = part if out is None else out + part
    return out


def _inverse_masks(r, s, c):
    same = lambda bs: (r // bs) == (s // bs)
    masks = [same(INV_BASE)]
    bs = INV_BASE
    while bs < c:
        masks.append(same(2 * bs) & jnp.logical_not(same(bs)))
        bs *= 2
    return [jnp.where(m, 1.0, 0.0).astype(BF16) for m in masks]


def _unit_lower_inverse(a_list, eye, masks, halves):
    c = a_list[0].shape[0]
    dot = lambda x, y: _pair_dot(x, y, halves)
    split = _split_bf16
    d = [a * masks[0] for a in a_list]
    p = [eye - di.astype(F32) for di in d]
    x = [dot((di,), (di,)) for di in d]
    both = [dot(split(jnp.concatenate([xi, pi], axis=0)), split(xi))
            for xi, pi in zip(x, p)]
    p = [pi + bi[c:] for pi, bi in zip(p, both)]
    p = [pi + dot(split(pi), split(bi[:c])) for pi, bi in zip(p, both)]
    for m in masks[1:]:
        ps = [split(pi) for pi in p]
        pm = [dot(psi, (a * m,)) for psi, a in zip(ps, a_list)]
        p = [pi - dot(split(pmi), psi) for pi, pmi, psi in zip(p, pm, ps)]
    return p


def _softplus(t):
    return jnp.maximum(t, 0.0) + jnp.log(1.0 + jnp.exp(-jnp.abs(t)))


def _cumsum_rows(lower, t):
    n = t.shape[1]
    parts = jnp.dot(lower.astype(BF16), jnp.concatenate(_split3_bf16(t), axis=1),
                    preferred_element_type=F32)
    return parts[:, :n] + parts[:, n:2 * n] + parts[:, 2 * n:]


def _cumsum_lanes(t, upper):
    m = t.shape[0]
    parts = jnp.dot(jnp.concatenate(_split3_bf16(t), axis=0), upper.astype(BF16),
                    preferred_element_type=F32)
    return parts[:m] + parts[m:2 * m] + parts[2 * m:]


def _dn_body(qkv_ref, gate_ref, ab_ref, shift_ref, cw_ref, alog_ref, dtb_ref,
             alogt_ref, dtbt_ref, nw_ref, o_ref, zbuf, state):
    ci = pl.program_id(1)
    c = DN_CHUNK
    nch = DN_STEP_CHUNKS
    rows = nch * c
    hk = DN_HEAD_K
    dv = DN_HEAD_V
    width = DN_WIDTH
    pad = BF16_ROWS
    hist = CONV_K - 1
    heads = range(DN_HEADS)
    pairs = range(DN_HEADS // 2)
    chunks = range(nch)

    @pl.when(ci == 0)
    def _():
        state[...] = jnp.zeros_like(state)
        zbuf[0:pad, :] = jnp.zeros((pad, 3 * width), BF16)

    zbuf[pad:pad + rows, :] = qkv_ref[...]
    shifted = jnp.dot(shift_ref[...], zbuf[...], preferred_element_type=F32)
    y = qkv_ref[...].astype(F32) * cw_ref[hist:hist + 1, :]
    for j in range(hist):
        y = y + shifted[j * rows:(j + 1) * rows] * cw_ref[j:j + 1, :]
    zbuf[0:pad, :] = zbuf[rows:rows + pad, :]
    y = _silu(y)

    n_qk = 2 * DN_HEADS
    sq = y[:, :2 * width] * y[:, :2 * width]
    sq = jnp.concatenate([sq[:, i * hk:(i + 1) * hk] for i in range(n_qk)], axis=0)
    sums = jnp.dot(jnp.concatenate(_split_bf16(sq), axis=0), jnp.ones((hk, hk), BF16),
                   preferred_element_type=F32)
    rnorm = lax.rsqrt(sums[:n_qk * rows] + sums[n_qk * rows:] + L2_EPS)

    r = lax.broadcasted_iota(jnp.int32, (c, 2 * c), 0)
    lane = lax.broadcasted_iota(jnp.int32, (c, 2 * c), 1)
    left = lane < c
    halves = (jnp.where(left, 1.0, 0.0).astype(BF16), jnp.where(left, 0.0, 1.0).astype(BF16))
    s = jnp.where(left, lane, lane - c)
    tri = r >= s
    strict = r > s
    eye = jnp.where(r == s, 1.0, 0.0).astype(F32)
    inv_masks = _inverse_masks(r, s, c)

    ab = ab_ref[...]
    g_col_raw = -jnp.exp(alog_ref[...]) * _softplus(ab + dtb_ref[...])
    beta_col = jax.nn.sigmoid(ab)
    rr = lax.broadcasted_iota(jnp.int32, (rows, rows), 0)
    ss = lax.broadcasted_iota(jnp.int32, (rows, rows), 1)
    lower = jnp.where((rr >= ss) & ((rr // c) == (ss // c)), 1.0, 0.0).astype(F32)
    g_col = _cumsum_rows(lower, g_col_raw)
    a_rows = ab.T[0:DN_HEADS]
    g_row_raw = -jnp.exp(alogt_ref[...]) * _softplus(a_rows + dtbt_ref[...])
    upper = jnp.where((rr <= ss) & ((rr // c) == (ss // c)), 1.0, 0.0).astype(F32)
    g_row = _cumsum_lanes(g_row_raw, upper)
    g_row_sw = pltpu.roll(g_row, c, axis=1)

    q, k, kb, vb, kbg, qg = [], [], [], [], [], []
    for h in heads:
        qh = y[:, h * hk:(h + 1) * hk]
        kh = y[:, width + h * hk:width + (h + 1) * hk]
        vh = y[:, 2 * width + h * dv:2 * width + (h + 1) * dv]
        qh = qh * rnorm[h * rows:(h + 1) * rows] * (hk ** -0.5)
        kh = kh * rnorm[(DN_HEADS + h) * rows:(DN_HEADS + h + 1) * rows]
        beta = beta_col[:, DN_HEADS + h:DN_HEADS + h + 1]
        eg = jnp.exp(g_col[:, h:h + 1])
        q.append(qh)
        k.append(kh)
        kb.append(kh * beta)
        vb.append(vh * beta)
        kbg.append(kb[h] * eg)
        qg.append(qh * eg)

    a_list, ai_list = [], []
    zeros_k = jnp.zeros((c, hk), F32)
    for cc in chunks:
        cs = slice(cc * c, (cc + 1) * c)
        for j in pairs:
            h0, h1 = 2 * j, 2 * j + 1
            gcp = jnp.where(left, jnp.broadcast_to(g_col[cs, h0:h0 + 1], (c, 2 * c)),
                            jnp.broadcast_to(g_col[cs, h1:h1 + 1], (c, 2 * c)))
            grp = (jnp.where(left[0:1], g_row[h0:h0 + 1], g_row_sw[h1:h1 + 1]) if cc == 0 else
                   jnp.where(left[0:1], g_row_sw[h0:h0 + 1], g_row[h1:h1 + 1]))
            decay = jnp.exp(jnp.where(tri, gcp - grp, -jnp.inf))
            lhs = jnp.concatenate([jnp.concatenate([kb[h0][cs], kb[h1][cs]], axis=1),
                                   jnp.concatenate([q[h0][cs], q[h1][cs]], axis=1)], axis=0)
            kbd = jnp.concatenate([jnp.concatenate([k[h0][cs], zeros_k], axis=1),
                                   jnp.concatenate([zeros_k, k[h1][cs]], axis=1)], axis=0)
            kq = lax.dot_general(lhs.astype(BF16), kbd.astype(BF16), (((1,), (1,)), ((), ())),
                                 preferred_element_type=F32)
            a_list.append(jnp.where(strict, kq[:c] * decay, 0.0).astype(BF16))
            ai_list.append(jnp.where(tri, kq[c:] * decay, 0.0).astype(BF16))

    t_list = _unit_lower_inverse(a_list, eye, inv_masks, halves)

    u = [[None] * DN_HEADS for _ in chunks]
    w = [[None] * DN_HEADS for _ in chunks]
    zeros_v = jnp.zeros((c, dv + hk), F32)
    for cc in chunks:
        cs = slice(cc * c, (cc + 1) * c)
        for j in pairs:
            h0, h1 = 2 * j, 2 * j + 1
            rhs = jnp.concatenate([jnp.concatenate([vb[h0][cs], kbg[h0][cs], zeros_v], axis=1),
                                   jnp.concatenate([zeros_v, vb[h1][cs], kbg[h1][cs]], axis=1)], axis=0)
            uw = jnp.dot(t_list[cc * len(pairs) + j].astype(BF16), rhs.astype(BF16),
                         preferred_element_type=F32)
            u[cc][h0], w[cc][h0] = uw[:, :dv], uw[:, dv:dv + hk]
            u[cc][h1], w[cc][h1] = uw[:, dv + hk:2 * dv + hk], uw[:, 2 * dv + hk:]

    st = [state[h] for h in heads]
    zeros_b = jnp.zeros((c, dv), BF16)
    for cc in chunks:
        cs = slice(cc * c, (cc + 1) * c)
        ws = [jnp.dot(jnp.concatenate([w[cc][h], qg[h][cs]], axis=0).astype(BF16), st[h].astype(BF16),
                      preferred_element_type=F32) for h in heads]
        v_new = [(u[cc][h] - ws[h][:c]).astype(BF16) for h in heads]
        intra = []
        for j in pairs:
            h0, h1 = 2 * j, 2 * j + 1
            vbd = jnp.concatenate([jnp.concatenate([v_new[h0], zeros_b], axis=1),
                                   jnp.concatenate([zeros_b, v_new[h1]], axis=1)], axis=0)
            intra.append(jnp.dot(ai_list[cc * len(pairs) + j], vbd, preferred_element_type=F32))
        for h in heads:
            j, half = divmod(h, 2)
            o = ws[h][c:] + intra[j][:, half * dv:(half + 1) * dv]
            gc = g_col[cs, h:h + 1]
            gl = gc[c - 1:c, :]
            kd = k[h][cs] * jnp.exp(gl - gc)
            st[h] = st[h] * jnp.exp(gl) + lax.dot_general(
                kd.astype(BF16), v_new[h], (((0,), (0,)), ((), ())), preferred_element_type=F32)
            var = jnp.mean(o * o, axis=-1, keepdims=True)
            on = o * lax.rsqrt(var + NORM_EPS) * nw_ref[...]
            hs = slice(h * dv, (h + 1) * dv)
            o_ref[cs, hs] = (on * _silu(gate_ref[cs, hs].astype(F32))).astype(o_ref.dtype)
    for h in heads:
        state[h] = st[h]


def _deltanet(proj, ab, conv_w, a_log, dt_bias, norm_w, batch, seq):
    t = batch * seq
    c = DN_CHUNK
    nch = DN_STEP_CHUNKS
    rows = nch * c
    ns = seq // rows
    width = DN_WIDTH
    row = lambda b, i: b * ns + i
    assert rows == LANES
    lane_pad = lambda p: jnp.pad(p.reshape(1, -1), ((0, 0), (0, LANES - p.shape[0])))
    const = lambda shape: pl.BlockSpec(shape, lambda b, i: (0,) * len(shape))
    hist = CONV_K - 1
    shift = np.zeros((hist * rows, BF16_ROWS + rows), np.float32)
    for j in range(hist):
        shift[j * rows + np.arange(rows), BF16_ROWS + np.arange(rows) - hist + j] = 1.0
    return pl.pallas_call(
        _dn_body,
        grid=(batch, ns),
        in_specs=[pl.BlockSpec((rows, 3 * width), lambda b, i: (row(b, i), COL_DN_QKV // (3 * width))),
                  pl.BlockSpec((rows, width), lambda b, i: (row(b, i), COL_DN_GATE // width)),
                  pl.BlockSpec((rows, LANES), lambda b, i: (row(b, i), 0)),
                  const((hist * rows, BF16_ROWS + rows)),
                  const((CONV_K, 3 * width)),
                  const((1, LANES)), const((1, LANES)),
                  const((DN_HEADS, 1)), const((DN_HEADS, 1)),
                  const((1, DN_HEAD_V))],
        out_specs=pl.BlockSpec((rows, width), lambda b, i: (row(b, i), 0)),
        out_shape=jax.ShapeDtypeStruct((t, width), BF16),
        scratch_shapes=[pltpu.VMEM((BF16_ROWS + rows, 3 * width), BF16),
                        pltpu.VMEM((DN_HEADS, DN_HEAD_K, DN_HEAD_V), F32)],
        compiler_params=_cparams(("arbitrary", "arbitrary")),
        name="dn",
    )(proj, proj, ab, jnp.asarray(shift, BF16), conv_w, lane_pad(a_log), lane_pad(dt_bias),
      a_log.reshape(-1, 1), dt_bias.reshape(-1, 1), norm_w.reshape(1, -1))


def _mix_out_body(h_ref, g_ref, a_ref, d_ref, wa_ref, wd_ref, o_ref):
    mixed = jnp.dot(a_ref[...], wa_ref[...], preferred_element_type=F32)
    mixed = mixed + jnp.dot(d_ref[...], wd_ref[...], preferred_element_type=F32)
    o_ref[...] = h_ref[...] + g_ref[0] * mixed


def _mix_out(h, mod3, seq, attn_out, dn_out, w_out):
    tm = TOKEN_TILE
    t, d = h.shape
    assert ATTN_WIDTH == DN_WIDTH
    return pl.pallas_call(
        _mix_out_body,
        grid=(t // tm,),
        in_specs=[pl.BlockSpec((tm, d), lambda i: (i, 0)),
                  pl.BlockSpec((1, 1, d), lambda i: (((i * tm) // seq) * N_MOD + 5, 0, 0)),
                  pl.BlockSpec((tm, ATTN_WIDTH), lambda i: (i, 0)),
                  pl.BlockSpec((tm, DN_WIDTH), lambda i: (i, 0)),
                  pl.BlockSpec((ATTN_WIDTH, d), lambda i: (0, 0)),
                  pl.BlockSpec((DN_WIDTH, d), lambda i: (1, 0))],
        out_specs=pl.BlockSpec((tm, d), lambda i: (i, 0)),
        out_shape=jax.ShapeDtypeStruct((t, d), F32),
        compiler_params=_cparams(("arbitrary",)),
        name="mix_out",
    )(h, mod3, attn_out, dn_out, w_out, w_out)


def kernel(x, c, positions, ada_w, ada_b, norm_ffn1, ffn1_w_gate, ffn1_w_up, ffn1_w_down, norm_mix, w_in, conv_w, a_log, dt_bias, attn_sinks, dn_norm_w, w_out, norm_ffn2, ffn2_w_gate, ffn2_w_up, ffn2_w_down, final_norm):
    batch, seq, d = x.shape
    depth = ada_w.shape[0]
    assert depth >= 1 and seq % ATTN_BLOCK == 0 and seq % TOKEN_TILE == 0
    assert seq % (DN_CHUNK * DN_STEP_CHUNKS) == 0
    t = batch * seq
    h = x.reshape(t, d)
    o_ak, o_av, o_dq = ATTN_WIDTH, ATTN_WIDTH + ATTN_KV_WIDTH, ATTN_WIDTH + 2 * ATTN_KV_WIDTH
    o_dg, o_ab = o_dq + 3 * DN_WIDTH, o_dq + 4 * DN_WIDTH
    assert COL_DN_AB % MIX_IN_TN == 0
    for l in range(depth):
        mod3 = _adaln(c, ada_w[l], ada_b[l]).reshape(batch * N_MOD, 1, d)
        h = _ffn(h, mod3, seq, 0, norm_ffn1[l], ffn1_w_gate[l], ffn1_w_up[l], ffn1_w_down[l],
                 final_norm, False)
        w = w_in[l]
        w_main = jnp.concatenate([w[:, o_dq:o_ab].astype(BF16), w[:, :o_dq].astype(BF16)], axis=1)
        w_ab = jnp.pad(w[:, o_ab:].astype(BF16), ((0, 0), (0, LANES - 2 * DN_HEADS)))
        proj, ab = _mix_in(h, mod3, seq, norm_mix[l], w_main, w_ab)
        attn_out = _attention(proj, positions, attn_sinks[l], batch, seq)
        dn_out = _deltanet(proj, ab, conv_w[l], a_log[l], dt_bias[l], dn_norm_w[l], batch, seq)
        h = _mix_out(h, mod3, seq, attn_out, dn_out, w_out[l].astype(BF16))
        h = _ffn(h, mod3, seq, 6, norm_ffn2[l], ffn2_w_gate[l], ffn2_w_up[l], ffn2_w_down[l],
                 final_norm, l == depth - 1)
    return h.reshape(batch, seq, d)
```

```python
import functools

import numpy as np
import jax
import jax.numpy as jnp
from jax import lax
from jax.experimental import pallas as pl
from jax.experimental.pallas import tpu as pltpu

F32 = jnp.float32
BF16 = jnp.bfloat16

ATTN_HEADS = 16
ATTN_KV_HEADS = 2
ATTN_HEAD_DIM = 64
WINDOW = 128
ATTN_BLOCK = 128
ROPE_THETA = 500000.0
ROPE_DIM = ATTN_HEAD_DIM // 4
DN_HEADS = 8
DN_HEAD_K = 128
DN_HEAD_V = 128
DN_CHUNK = 64
CONV_K = 4
NORM_EPS = 1e-6
L2_EPS = 1e-6
N_MOD = 9

ATTN_WIDTH = ATTN_HEADS * ATTN_HEAD_DIM
ATTN_KV_WIDTH = ATTN_KV_HEADS * ATTN_HEAD_DIM
DN_WIDTH = DN_HEADS * DN_HEAD_V
IN_WIDTH = ATTN_WIDTH + 2 * ATTN_KV_WIDTH + 4 * DN_WIDTH + 2 * DN_HEADS
COL_DN_QKV = 0
COL_DN_GATE = 3 * DN_WIDTH
COL_ATTN_Q = 4 * DN_WIDTH
COL_ATTN_K = COL_ATTN_Q + ATTN_WIDTH
COL_ATTN_V = COL_ATTN_K + ATTN_KV_WIDTH
COL_DN_AB = COL_ATTN_V + ATTN_KV_WIDTH
LANES = 128
SUBLANES = 8
BF16_ROWS = 16
VMEM_LIMIT = 56 * 1024 * 1024
FFN_VMEM_LIMIT = 60 * 1024 * 1024
TOKEN_TILE = 512
FFN_TM = 1024
FFN_TF = 256
MIX_IN_TM = 1024
MIX_IN_TN = 1792
ADALN_TN = 1024
W_IN_ROWS = 256
NORM_ROWS = 128


def _cparams(sem, vmem_limit=VMEM_LIMIT):
    return pltpu.CompilerParams(dimension_semantics=sem, vmem_limit_bytes=vmem_limit)


def _silu(t):
    return t * jax.nn.sigmoid(t)


def _norm_mod_to(u_ref, h_ref, nw, shift, scale):
    gain = nw * (1.0 + scale)
    for r in range(0, h_ref.shape[0], NORM_ROWS):
        h = h_ref[r:r + NORM_ROWS, :]
        var = jnp.mean(h * h, axis=-1, keepdims=True)
        u_ref[r:r + NORM_ROWS, :] = (h * lax.rsqrt(var + NORM_EPS) * gain + shift).astype(u_ref.dtype)


def _adaln_body(c_ref, w_ref, b_ref, o_ref):
    ca = _silu(c_ref[...]).astype(BF16)
    o_ref[...] = jnp.dot(ca, w_ref[...].astype(BF16), preferred_element_type=F32) + b_ref[...]


def _adaln(c, w, b):
    tn = ADALN_TN
    nb, d = c.shape
    n = w.shape[1]
    rows = -(-nb // SUBLANES) * SUBLANES
    c_pad = jnp.pad(c, ((0, rows - nb), (0, 0)))
    out = pl.pallas_call(
        _adaln_body,
        grid=(n // tn,),
        in_specs=[pl.BlockSpec((rows, d), lambda j: (0, 0)),
                  pl.BlockSpec((d, tn), lambda j: (0, j)),
                  pl.BlockSpec((1, tn), lambda j: (0, j))],
        out_specs=pl.BlockSpec((rows, tn), lambda j: (0, j)),
        out_shape=jax.ShapeDtypeStruct((rows, n), F32),
        compiler_params=_cparams(("arbitrary",)),
        name="adaln",
    )(c_pad, w, b.reshape(1, n))
    return out[:nb]


def _ffn_body(h_ref, nw_ref, sh_ref, sc_ref, g_ref, wg_ref, wu_ref, wd_ref, fn_ref,
              o_ref, u_scr, *, final_norm):
    j = pl.program_id(1)

    @pl.when(j == 0)
    def _():
        _norm_mod_to(u_scr, h_ref, nw_ref[...], sh_ref[0], sc_ref[0])
        o_ref[...] = jnp.zeros_like(o_ref)

    u = u_scr[...]
    gate = jnp.dot(u, wg_ref[...].astype(BF16), preferred_element_type=F32)
    up = jnp.dot(u, wu_ref[...].astype(BF16), preferred_element_type=F32)
    act = (_silu(gate) * up).astype(BF16)
    o_ref[...] += jnp.dot(act, wd_ref[...].astype(BF16), preferred_element_type=F32)

    @pl.when(j == pl.num_programs(1) - 1)
    def _():
        hn = h_ref[...] + 0.5 * g_ref[0] * o_ref[...]
        if final_norm:
            var = jnp.mean(hn * hn, axis=-1, keepdims=True)
            hn = hn * lax.rsqrt(var + NORM_EPS) * fn_ref[...]
        o_ref[...] = hn


def _ffn(h, mod3, seq, k0, nw, wg, wu, wd, fn, final_norm):
    tm, tf = FFN_TM, FFN_TF
    t, d = h.shape
    dff = wg.shape[1]
    mod_spec = lambda k: pl.BlockSpec((1, 1, d), lambda i, j: (((i * tm) // seq) * N_MOD + k, 0, 0))
    return pl.pallas_call(
        functools.partial(_ffn_body, final_norm=final_norm),
        grid=(t // tm, dff // tf),
        in_specs=[pl.BlockSpec((tm, d), lambda i, j: (i, 0)),
                  pl.BlockSpec((1, d), lambda i, j: (0, 0)),
                  mod_spec(k0), mod_spec(k0 + 1), mod_spec(k0 + 2),
                  pl.BlockSpec((d, tf), lambda i, j: (0, j)),
                  pl.BlockSpec((d, tf), lambda i, j: (0, j)),
                  pl.BlockSpec((tf, d), lambda i, j: (j, 0)),
                  pl.BlockSpec((1, d), lambda i, j: (0, 0))],
        out_specs=pl.BlockSpec((tm, d), lambda i, j: (i, 0)),
        out_shape=jax.ShapeDtypeStruct((t, d), F32),
        scratch_shapes=[pltpu.VMEM((tm, d), BF16)],
        compiler_params=_cparams(("arbitrary", "arbitrary"), FFN_VMEM_LIMIT),
        name="ffn_final" if final_norm else "ffn",
    )(h, nw.reshape(1, d), mod3, mod3, mod3, wg, wu, wd, fn.reshape(1, d))


def _prep_w_in_body(w_ref, main_ref, ab_ref):
    o_dq = ATTN_WIDTH + 2 * ATTN_KV_WIDTH
    o_ab = o_dq + 4 * DN_WIDTH
    main_ref[:, :COL_ATTN_Q] = w_ref[:, o_dq:o_ab].astype(BF16)
    main_ref[:, COL_ATTN_Q:] = w_ref[:, :o_dq].astype(BF16)
    ab_ref[...] = jnp.zeros_like(ab_ref)
    ab_ref[:, :2 * DN_HEADS] = w_ref[:, o_ab:].astype(BF16)


def _prep_w_in(w):
    d, n = w.shape
    assert n == IN_WIDTH
    return pl.pallas_call(
        _prep_w_in_body,
        grid=(d // W_IN_ROWS,),
        in_specs=[pl.BlockSpec((W_IN_ROWS, n), lambda i: (i, 0))],
        out_specs=[pl.BlockSpec((W_IN_ROWS, COL_DN_AB), lambda i: (i, 0)),
                   pl.BlockSpec((W_IN_ROWS, LANES), lambda i: (i, 0))],
        out_shape=[jax.ShapeDtypeStruct((d, COL_DN_AB), BF16),
                   jax.ShapeDtypeStruct((d, LANES), BF16)],
        compiler_params=_cparams(("arbitrary",)),
        name="w_in",
    )(w)


def _mix_in_body(h_ref, nw_ref, sh_ref, sc_ref, w_ref, wab_ref, o_ref, ab_ref, u_scr):
    j = pl.program_id(1)

    @pl.when(j == 0)
    def _():
        _norm_mod_to(u_scr, h_ref, nw_ref[...], sh_ref[0], sc_ref[0])
        ab_ref[...] = jnp.dot(u_scr[...], wab_ref[...], preferred_element_type=F32)

    o_ref[...] = jnp.dot(u_scr[...], w_ref[...], preferred_element_type=F32).astype(o_ref.dtype)


def _mix_in(h, mod3, seq, nw, w_main, wab):
    tm, tn = MIX_IN_TM, MIX_IN_TN
    t, d = h.shape
    n = w_main.shape[1]
    mod_spec = lambda k: pl.BlockSpec((1, 1, d), lambda i, j: (((i * tm) // seq) * N_MOD + k, 0, 0))
    return pl.pallas_call(
        _mix_in_body,
        grid=(t // tm, n // tn),
        in_specs=[pl.BlockSpec((tm, d), lambda i, j: (i, 0)),
                  pl.BlockSpec((1, d), lambda i, j: (0, 0)),
                  mod_spec(3), mod_spec(4),
                  pl.BlockSpec((d, tn), lambda i, j: (0, j)),
                  pl.BlockSpec((d, LANES), lambda i, j: (0, 0))],
        out_specs=[pl.BlockSpec((tm, tn), lambda i, j: (i, j)),
                   pl.BlockSpec((tm, LANES), lambda i, j: (i, 0))],
        out_shape=[jax.ShapeDtypeStruct((t, n), BF16),
                   jax.ShapeDtypeStruct((t, LANES), F32)],
        scratch_shapes=[pltpu.VMEM((tm, d), BF16)],
        compiler_params=_cparams(("arbitrary", "arbitrary")),
        name="mix_in",
    )(h, nw.reshape(1, d), mod3, mod3, w_main, wab)


def _attn_body(sink_ref, pos_ref, invf_ref, rot_ref, q_ref, k_ref, v_ref, o_ref, kprev, vprev):
    n = pl.program_id(1)
    blk = ATTN_BLOCK
    hd = ATTN_HEAD_DIM
    pairs_per_kv = ATTN_HEADS // ATTN_KV_HEADS // 2
    n_pairs = ATTN_HEADS // 2

    @pl.when(n == 0)
    def _():
        kprev[...] = jnp.zeros_like(kprev)
        vprev[...] = jnp.zeros_like(vprev)

    ang = pos_ref[...].astype(F32) * invf_ref[...]
    cos = jnp.cos(ang)
    sin = jnp.sin(ang)
    rot = rot_ref[...]

    kin = k_ref[...]
    k = kin.astype(F32) * cos + jnp.dot(kin, rot, preferred_element_type=F32) * sin
    qin = jnp.concatenate([q_ref[:, p * LANES:(p + 1) * LANES] for p in range(n_pairs)], axis=0)
    cos_q = jnp.concatenate([cos] * n_pairs, axis=0)
    sin_q = jnp.concatenate([sin] * n_pairs, axis=0)
    q = (qin.astype(F32) * cos_q + jnp.dot(qin, rot, preferred_element_type=F32) * sin_q) * (hd ** -0.5)
    q = q.astype(BF16)

    lane2 = lax.broadcasted_iota(jnp.int32, (2 * blk, LANES), 1)
    left2 = jnp.where(lane2 < hd, 1.0, 0.0).astype(BF16)
    right2 = jnp.where(lane2 < hd, 0.0, 1.0).astype(BF16)
    v = v_ref[...]
    cur = [k.astype(BF16), pltpu.roll(k, hd, axis=1).astype(BF16)]
    cur_v = [v, pltpu.roll(v.astype(F32), hd, axis=1).astype(BF16)]
    kband = [jnp.concatenate([kprev[i], cur[i]], axis=0) for i in range(2)]
    vband = [jnp.concatenate([vprev[i], cur_v[i]], axis=0) for i in range(2)]
    ones_cols = jnp.concatenate([left2, right2], axis=0)

    qi = lax.broadcasted_iota(jnp.int32, (blk, blk), 0)
    kj = lax.broadcasted_iota(jnp.int32, (blk, blk), 1)
    upper = kj > qi
    left = kj < hd
    prev_bias = jnp.where(n > 0, 0.0, -jnp.inf)

    scores = []
    for g in range(ATTN_KV_HEADS):
        a, b = (0, 1) if g == 0 else (1, 0)
        kbd = jnp.concatenate([kband[a] * left2, kband[b] * right2], axis=0)
        qg = q[g * pairs_per_kv * blk:(g + 1) * pairs_per_kv * blk]
        scores.append(lax.dot_general(qg, kbd, (((1,), (1,)), ((), ())), preferred_element_type=F32))

    probs, mx = [], []
    for p in range(n_pairs):
        g, pp = divmod(p, pairs_per_kv)
        sp = scores[g][pp * blk:(pp + 1) * blk]
        tiles, ms = [], []
        for i in range(2):
            comb = jnp.where(upper, sp[:, 2 * i * blk:(2 * i + 1) * blk] + prev_bias,
                             sp[:, (2 * i + 1) * blk:(2 * i + 2) * blk])
            m = jnp.maximum(jnp.max(comb, axis=-1, keepdims=True), sink_ref[2 * p + i])
            e = jnp.exp(comb - m)
            tiles += [jnp.where(upper, e, 0.0).astype(BF16), jnp.where(upper, 0.0, e).astype(BF16)]
            ms.append(m)
        probs.append(jnp.concatenate(tiles, axis=1))
        mx.append(ms)

    for g in range(ATTN_KV_HEADS):
        a, b = (0, 1) if g == 0 else (1, 0)
        vbd = jnp.concatenate([vband[a] * left2, vband[b] * right2], axis=0)
        vext = jnp.concatenate([vbd, ones_cols], axis=1)
        pg = jnp.concatenate(probs[g * pairs_per_kv:(g + 1) * pairs_per_kv], axis=0)
        res = jnp.dot(pg, vext, preferred_element_type=F32)
        for pp in range(pairs_per_kv):
            p = g * pairs_per_kv + pp
            rp = res[pp * blk:(pp + 1) * blk]
            sink_mass = jnp.where(left, jnp.exp(sink_ref[2 * p] - mx[p][0]),
                                  jnp.exp(sink_ref[2 * p + 1] - mx[p][1]))
            o = rp[:, :LANES] / (rp[:, LANES:] + sink_mass)
            o_ref[:, p * LANES:(p + 1) * LANES] = o.astype(o_ref.dtype)

    for i in range(2):
        kprev[i] = cur[i]
        vprev[i] = cur_v[i]


def _attention(proj, positions, sinks, batch, seq):
    assert WINDOW == ATTN_BLOCK and 2 * ATTN_HEAD_DIM == LANES
    t = batch * seq
    nb = seq // ATTN_BLOCK
    blk = ATTN_BLOCK
    half = ROPE_DIM // 2
    inv_freq = ROPE_THETA ** (-jnp.arange(0, ROPE_DIM, 2, dtype=F32) / ROPE_DIM)
    head_pat = jnp.concatenate([inv_freq, inv_freq, jnp.zeros((ATTN_HEAD_DIM - 2 * half,), F32)])
    invf = jnp.tile(head_pat, LANES // ATTN_HEAD_DIM).reshape(1, LANES)
    rot = np.zeros((LANES, LANES), np.float32)
    for base in range(0, LANES, ATTN_HEAD_DIM):
        for d in range(half):
            rot[base + d + half, base + d] = -1.0
            rot[base + d, base + d + half] = 1.0
    kcol = COL_ATTN_K // LANES
    row = lambda b, n: b * nb + n
    return pl.pallas_call(
        _attn_body,
        grid=(batch, nb),
        in_specs=[pl.BlockSpec(memory_space=pltpu.SMEM),
                  pl.BlockSpec((blk, 1), lambda b, n: (row(b, n), 0)),
                  pl.BlockSpec((1, LANES), lambda b, n: (0, 0)),
                  pl.BlockSpec((LANES, LANES), lambda b, n: (0, 0)),
                  pl.BlockSpec((blk, ATTN_WIDTH), lambda b, n: (row(b, n), COL_ATTN_Q // ATTN_WIDTH)),
                  pl.BlockSpec((blk, LANES), lambda b, n: (row(b, n), kcol)),
                  pl.BlockSpec((blk, LANES), lambda b, n: (row(b, n), kcol + 1))],
        out_specs=pl.BlockSpec((blk, ATTN_WIDTH), lambda b, n: (row(b, n), 0)),
        out_shape=jax.ShapeDtypeStruct((t, ATTN_WIDTH), BF16),
        scratch_shapes=[pltpu.VMEM((2, blk, LANES), BF16), pltpu.VMEM((2, blk, LANES), BF16)],
        compiler_params=_cparams(("arbitrary", "arbitrary")),
        name="attn",
    )(sinks, positions.reshape(t, 1), invf, jnp.asarray(rot, BF16), proj, proj, proj)


INV_BASE = 8
DN_STEP_CHUNKS = 2


def _split_bf16(t):
    hi = t.astype(BF16)
    lo = (t - hi.astype(F32)).astype(BF16)
    return hi, lo


def _split3_bf16(t):
    hi = t.astype(BF16)
    r1 = t - hi.astype(F32)
    mid = r1.astype(BF16)
    lo = (r1 - mid.astype(F32)).astype(BF16)
    return hi, mid, lo


def _pair_blockdiag(t, halves):
    return jnp.concatenate([t * halves[0], t * halves[1]], axis=0)


def _pair_dot(a_terms, b_terms, halves):
    m = a_terms[0].shape[0]
    out = None
    for j, b in enumerate(b_terms):
        lhs = a_terms if j == 0 else a_terms[:1]
        prod = jnp.dot(jnp.concatenate(lhs, axis=0), _pair_blockdiag(b, halves),
                       preferred_element_type=F32)
        for i in range(len(lhs)):
            part = prod[i * m:(i + 1) * m]
            out = part if out is None else out + part
    return out


def _inverse_masks(r, s, c):
    same = lambda bs: (r // bs) == (s // bs)
    masks = [same(INV_BASE)]
    bs = INV_BASE
    while bs < c:
        masks.append(same(2 * bs) & jnp.logical_not(same(bs)))
        bs *= 2
    return [jnp.where(m, 1.0, 0.0).astype(BF16) for m in masks]


def _unit_lower_inverse(a_list, eye, masks, halves):
    c = a_list[0].shape[0]
    dot = lambda x, y: _pair_dot(x, y, halves)
    split = _split_bf16
    d = [a * masks[0] for a in a_list]
    p = [eye - di.astype(F32) for di in d]
    x = [dot((di,), (di,)) for di in d]
    both = [dot(split(jnp.concatenate([xi, pi], axis=0)), split(xi))
            for xi, pi in zip(x, p)]
    p = [pi + bi[c:] for pi, bi in zip(p, both)]
    p = [pi + dot(split(pi), split(bi[:c])) for pi, bi in zip(p, both)]
    for m in masks[1:]:
        ps = [split(pi) for pi in p]
        pm = [dot(psi, (a * m,)) for psi, a in zip(ps, a_list)]
        p = [pi - dot(split(pmi), psi) for pi, pmi, psi in zip(p, pm, ps)]
    return p


def _softplus(t):
    return jnp.maximum(t, 0.0) + jnp.log(1.0 + jnp.exp(-jnp.abs(t)))


def _cumsum_rows(lower, t):
    n = t.shape[1]
    parts = jnp.dot(lower.astype(BF16), jnp.concatenate(_split3_bf16(t), axis=1),
                    preferred_element_type=F32)
    return parts[:, :n] + parts[:, n:2 * n] + parts[:, 2 * n:]


def _cumsum_lanes(t, upper):
    m = t.shape[0]
    parts = jnp.dot(jnp.concatenate(_split3_bf16(t), axis=0), upper.astype(BF16),
                    preferred_element_type=F32)
    return parts[:m] + parts[m:2 * m] + parts[2 * m:]


def _dn_body(qkv_ref, gate_ref, ab_ref, shift_ref, cw_ref, alog_ref, dtb_ref,
             alogt_ref, dtbt_ref, nw_ref, o_ref, zbuf, state):
    ci = pl.program_id(1)
    c = DN_CHUNK
    nch = DN_STEP_CHUNKS
    rows = nch * c
    hk = DN_HEAD_K
    dv = DN_HEAD_V
    width = DN_WIDTH
    pad = BF16_ROWS
    hist = CONV_K - 1
    heads = range(DN_HEADS)
    pairs = range(DN_HEADS // 2)
    chunks = range(nch)

    @pl.when(ci == 0)
    def _():
        state[...] = jnp.zeros_like(state)
        zbuf[0:pad, :] = jnp.zeros((pad, 3 * width), BF16)

    zbuf[pad:pad + rows, :] = qkv_ref[...]
    shifted = jnp.dot(shift_ref[...], zbuf[...], preferred_element_type=F32)
    y = qkv_ref[...].astype(F32) * cw_ref[hist:hist + 1, :]
    for j in range(hist):
        y = y + shifted[j * rows:(j + 1) * rows] * cw_ref[j:j + 1, :]
    zbuf[0:pad, :] = zbuf[rows:rows + pad, :]
    y = _silu(y)

    n_qk = 2 * DN_HEADS
    sq = y[:, :2 * width] * y[:, :2 * width]
    sq = jnp.concatenate([sq[:, i * hk:(i + 1) * hk] for i in range(n_qk)], axis=0)
    sums = jnp.dot(jnp.concatenate(_split_bf16(sq), axis=0), jnp.ones((hk, hk), BF16),
                   preferred_element_type=F32)
    rnorm = lax.rsqrt(sums[:n_qk * rows] + sums[n_qk * rows:] + L2_EPS)

    r = lax.broadcasted_iota(jnp.int32, (c, 2 * c), 0)
    lane = lax.broadcasted_iota(jnp.int32, (c, 2 * c), 1)
    left = lane < c
    halves = (jnp.where(left, 1.0, 0.0).astype(BF16), jnp.where(left, 0.0, 1.0).astype(BF16))
    s = jnp.where(left, lane, lane - c)
    tri = r >= s
    strict = r > s
    eye = jnp.where(r == s, 1.0, 0.0).astype(F32)
    inv_masks = _inverse_masks(r, s, c)

    ab = ab_ref[...]
    g_col_raw = -jnp.exp(alog_ref[...]) * _softplus(ab + dtb_ref[...])
    beta_col = jax.nn.sigmoid(ab)
    rr = lax.broadcasted_iota(jnp.int32, (rows, rows), 0)
    ss = lax.broadcasted_iota(jnp.int32, (rows, rows), 1)
    lower = jnp.where((rr >= ss) & ((rr // c) == (ss // c)), 1.0, 0.0).astype(F32)
    g_col = _cumsum_rows(lower, g_col_raw)
    a_rows = ab.T[0:DN_HEADS]
    g_row_raw = -jnp.exp(alogt_ref[...]) * _softplus(a_rows + dtbt_ref[...])
    upper = jnp.where((rr <= ss) & ((rr // c) == (ss // c)), 1.0, 0.0).astype(F32)
    g_row = _cumsum_lanes(g_row_raw, upper)
    g_row_sw = pltpu.roll(g_row, c, axis=1)

    q, k, kb, vb, kbg, qg = [], [], [], [], [], []
    for h in heads:
        qh = y[:, h * hk:(h + 1) * hk]
        kh = y[:, width + h * hk:width + (h + 1) * hk]
        vh = y[:, 2 * width + h * dv:2 * width + (h + 1) * dv]
        qh = qh * rnorm[h * rows:(h + 1) * rows] * (hk ** -0.5)
        kh = kh * rnorm[(DN_HEADS + h) * rows:(DN_HEADS + h + 1) * rows]
        beta = beta_col[:, DN_HEADS + h:DN_HEADS + h + 1]
        eg = jnp.exp(g_col[:, h:h + 1])
        q.append(qh)
        k.append(kh)
        kb.append(kh * beta)
        vb.append(vh * beta)
        kbg.append(kb[h] * eg)
        qg.append(qh * eg)

    a_list, ai_list = [], []
    zeros_k = jnp.zeros((c, hk), F32)
    for cc in chunks:
        cs = slice(cc * c, (cc + 1) * c)
        for j in pairs:
            h0, h1 = 2 * j, 2 * j + 1
            gcp = jnp.where(left, jnp.broadcast_to(g_col[cs, h0:h0 + 1], (c, 2 * c)),
                            jnp.broadcast_to(g_col[cs, h1:h1 + 1], (c, 2 * c)))
            grp = (jnp.where(left[0:1], g_row[h0:h0 + 1], g_row_sw[h1:h1 + 1]) if cc == 0 else
                   jnp.where(left[0:1], g_row_sw[h0:h0 + 1], g_row[h1:h1 + 1]))
            decay = jnp.exp(jnp.where(tri, gcp - grp, -jnp.inf))
            lhs = jnp.concatenate([jnp.concatenate([kb[h0][cs], kb[h1][cs]], axis=1),
                                   jnp.concatenate([q[h0][cs], q[h1][cs]], axis=1)], axis=0)
            kbd = jnp.concatenate([jnp.concatenate([k[h0][cs], zeros_k], axis=1),
                                   jnp.concatenate([zeros_k, k[h1][cs]], axis=1)], axis=0)
            kq = lax.dot_general(lhs.astype(BF16), kbd.astype(BF16), (((1,), (1,)), ((), ())),
                                 preferred_element_type=F32)
            a_list.append(jnp.where(strict, kq[:c] * decay, 0.0).astype(BF16))
            ai_list.append(jnp.where(tri, kq[c:] * decay, 0.0).astype(BF16))

    t_list = _unit_lower_inverse(a_list, eye, inv_masks, halves)

    u = [[None] * DN_HEADS for _ in chunks]
    w = [[None] * DN_HEADS for _ in chunks]
    zeros_v = jnp.zeros((c, dv + hk), F32)
    for cc in chunks:
        cs = slice(cc * c, (cc + 1) * c)
        for j in pairs:
            h0, h1 = 2 * j, 2 * j + 1
            rhs = jnp.concatenate([jnp.concatenate([vb[h0][cs], kbg[h0][cs], zeros_v], axis=1),
                                   jnp.concatenate([zeros_v, vb[h1][cs], kbg[h1][cs]], axis=1)], axis=0)
            uw = jnp.dot(t_list[cc * len(pairs) + j].astype(BF16), rhs.astype(BF16),
                         preferred_element_type=F32)
            u[cc][h0], w[cc][h0] = uw[:, :dv], uw[:, dv:dv + hk]
            u[cc][h1], w[cc][h1] = uw[:, dv + hk:2 * dv + hk], uw[:, 2 * dv + hk:]

    st = [state[h] for h in heads]
    zeros_b = jnp.zeros((c, dv), BF16)
    for cc in chunks:
        cs = slice(cc * c, (cc + 1) * c)
        ws = [jnp.dot(jnp.concatenate([w[cc][h], qg[h][cs]], axis=0).astype(BF16), st[h].astype(BF16),
                      preferred_element_type=F32) for h in heads]
        v_new = [(u[cc][h] - ws[h][:c]).astype(BF16) for h in heads]
        intra = []
        for j in pairs:
            h0, h1 = 2 * j, 2 * j + 1
            vbd = jnp.concatenate([jnp.concatenate([v_new[h0], zeros_b], axis=1),
                                   jnp.concatenate([zeros_b, v_new[h1]], axis=1)], axis=0)
            intra.append(jnp.dot(ai_list[cc * len(pairs) + j], vbd, preferred_element_type=F32))
        for h in heads:
            j, half = divmod(h, 2)
            o = ws[h][c:] + intra[j][:, half * dv:(half + 1) * dv]
            gc = g_col[cs, h:h + 1]
            gl = gc[c - 1:c, :]
            kd = k[h][cs] * jnp.exp(gl - gc)
            st[h] = st[h] * jnp.exp(gl) + lax.dot_general(
                kd.astype(BF16), v_new[h], (((0,), (0,)), ((), ())), preferred_element_type=F32)
            var = jnp.mean(o * o, axis=-1, keepdims=True)
            on = o * lax.rsqrt(var + NORM_EPS) * nw_ref[...]
            hs = slice(h * dv, (h + 1) * dv)
            o_ref[cs, hs] = (on * _silu(gate_ref[cs, hs].astype(F32))).astype(o_ref.dtype)
    for h in heads:
        state[h] = st[h]


def _deltanet(proj, ab, conv_w, a_log, dt_bias, norm_w, batch, seq):
    t = batch * seq
    c = DN_CHUNK
    nch = DN_STEP_CHUNKS
    rows = nch * c
    ns = seq // rows
    width = DN_WIDTH
    row = lambda b, i: b * ns + i
    assert rows == LANES
    lane_pad = lambda p: jnp.pad(p.reshape(1, -1), ((0, 0), (0, LANES - p.shape[0])))
    const = lambda shape: pl.BlockSpec(shape, lambda b, i: (0,) * len(shape))
    hist = CONV_K - 1
    shift = np.zeros((hist * rows, BF16_ROWS + rows), np.float32)
    for j in range(hist):
        shift[j * rows + np.arange(rows), BF16_ROWS + np.arange(rows) - hist + j] = 1.0
    return pl.pallas_call(
        _dn_body,
        grid=(batch, ns),
        in_specs=[pl.BlockSpec((rows, 3 * width), lambda b, i: (row(b, i), COL_DN_QKV // (3 * width))),
                  pl.BlockSpec((rows, width), lambda b, i: (row(b, i), COL_DN_GATE // width)),
                  pl.BlockSpec((rows, LANES), lambda b, i: (row(b, i), 0)),
                  const((hist * rows, BF16_ROWS + rows)),
                  const((CONV_K, 3 * width)),
                  const((1, LANES)), const((1, LANES)),
                  const((DN_HEADS, 1)), const((DN_HEADS, 1)),
                  const((1, DN_HEAD_V))],
        out_specs=pl.BlockSpec((rows, width), lambda b, i: (row(b, i), 0)),
        out_shape=jax.ShapeDtypeStruct((t, width), BF16),
        scratch_shapes=[pltpu.VMEM((BF16_ROWS + rows, 3 * width), BF16),
                        pltpu.VMEM((DN_HEADS, DN_HEAD_K, DN_HEAD_V), F32)],
        compiler_params=_cparams(("arbitrary", "arbitrary")),
        name="dn",
    )(proj, proj, ab, jnp.asarray(shift, BF16), conv_w, lane_pad(a_log), lane_pad(dt_bias),
      a_log.reshape(-1, 1), dt_bias.reshape(-1, 1), norm_w.reshape(1, -1))


def _mix_out_body(h_ref, g_ref, a_ref, d_ref, wa_ref, wd_ref, o_ref):
    mixed = jnp.dot(a_ref[...], wa_ref[...], preferred_element_type=F32)
    mixed = mixed + jnp.dot(d_ref[...], wd_ref[...], preferred_element_type=F32)
    o_ref[...] = h_ref[...] + g_ref[0] * mixed


def _mix_out(h, mod3, seq, attn_out, dn_out, w_out):
    tm = TOKEN_TILE
    t, d = h.shape
    assert ATTN_WIDTH == DN_WIDTH
    return pl.pallas_call(
        _mix_out_body,
        grid=(t // tm,),
        in_specs=[pl.BlockSpec((tm, d), lambda i: (i, 0)),
                  pl.BlockSpec((1, 1, d), lambda i: (((i * tm) // seq) * N_MOD + 5, 0, 0)),
                  pl.BlockSpec((tm, ATTN_WIDTH), lambda i: (i, 0)),
                  pl.BlockSpec((tm, DN_WIDTH), lambda i: (i, 0)),
                  pl.BlockSpec((ATTN_WIDTH, d), lambda i: (0, 0)),
                  pl.BlockSpec((DN_WIDTH, d), lambda i: (1, 0))],
        out_specs=pl.BlockSpec((tm, d), lambda i: (i, 0)),
        out_shape=jax.ShapeDtypeStruct((t, d), F32),
        compiler_params=_cparams(("arbitrary",)),
        name="mix_out",
    )(h, mod3, attn_out, dn_out, w_out, w_out)


def kernel(x, c, positions, ada_w, ada_b, norm_ffn1, ffn1_w_gate, ffn1_w_up, ffn1_w_down, norm_mix, w_in, conv_w, a_log, dt_bias, attn_sinks, dn_norm_w, w_out, norm_ffn2, ffn2_w_gate, ffn2_w_up, ffn2_w_down, final_norm):
    batch, seq, d = x.shape
    depth = ada_w.shape[0]
    assert depth >= 1 and seq % ATTN_BLOCK == 0 and seq % TOKEN_TILE == 0
    assert seq % (DN_CHUNK * DN_STEP_CHUNKS) == 0
    t = batch * seq
    h = x.reshape(t, d)
    assert seq % FFN_TM == 0 and seq % MIX_IN_TM == 0 and COL_DN_AB % MIX_IN_TN == 0
    for l in range(depth):
        mod3 = _adaln(c, ada_w[l], ada_b[l]).reshape(batch * N_MOD, 1, d)
        h = _ffn(h, mod3, seq, 0, norm_ffn1[l], ffn1_w_gate[l], ffn1_w_up[l], ffn1_w_down[l],
                 final_norm, False)
        w_main, w_ab = _prep_w_in(w_in[l])
        proj, ab = _mix_in(h, mod3, seq, norm_mix[l], w_main, w_ab)
        attn_out = _attention(proj, positions, attn_sinks[l], batch, seq)
        dn_out = _deltanet(proj, ab, conv_w[l], a_log[l], dt_bias[l], dn_norm_w[l], batch, seq)
        h = _mix_out(h, mod3, seq, attn_out, dn_out, w_out[l].astype(BF16))
        h = _ffn(h, mod3, seq, 6, norm_ffn2[l], ffn2_w_gate[l], ffn2_w_up[l], ffn2_w_down[l],
                 final_norm, l == depth - 1)
    return h.reshape(batch, seq, d)
```

```python
import functools

import numpy as np
import jax
import jax.numpy as jnp
from jax import lax
from jax.experimental import pallas as pl
from jax.experimental.pallas import tpu as pltpu

F32 = jnp.float32
BF16 = jnp.bfloat16

ATTN_HEADS = 16
ATTN_KV_HEADS = 2
ATTN_HEAD_DIM = 64
WINDOW = 128
ATTN_BLOCK = 128
ROPE_THETA = 500000.0
ROPE_DIM = ATTN_HEAD_DIM // 4
DN_HEADS = 8
DN_HEAD_K = 128
DN_HEAD_V = 128
DN_CHUNK = 64
CONV_K = 4
NORM_EPS = 1e-6
L2_EPS = 1e-6
N_MOD = 9

ATTN_WIDTH = ATTN_HEADS * ATTN_HEAD_DIM
ATTN_KV_WIDTH = ATTN_KV_HEADS * ATTN_HEAD_DIM
DN_WIDTH = DN_HEADS * DN_HEAD_V
IN_WIDTH = ATTN_WIDTH + 2 * ATTN_KV_WIDTH + 4 * DN_WIDTH + 2 * DN_HEADS
COL_DN_QKV = 0
COL_DN_GATE = 3 * DN_WIDTH
COL_ATTN_Q = 4 * DN_WIDTH
COL_ATTN_K = COL_ATTN_Q + ATTN_WIDTH
COL_ATTN_V = COL_ATTN_K + ATTN_KV_WIDTH
COL_DN_AB = COL_ATTN_V + ATTN_KV_WIDTH
LANES = 128
SUBLANES = 8
BF16_ROWS = 16
VMEM_LIMIT = 56 * 1024 * 1024
FFN_VMEM_LIMIT = 60 * 1024 * 1024
TOKEN_TILE = 512
FFN_TM = 1024
FFN_TF = 256
MIX_IN_TM = 1024
MIX_IN_TN = 1792
ADALN_TN = 1024
NORM_ROWS = 128


def _cparams(sem, vmem_limit=VMEM_LIMIT):
    return pltpu.CompilerParams(dimension_semantics=sem, vmem_limit_bytes=vmem_limit)


def _silu(t):
    return t * jax.nn.sigmoid(t)


def _norm_mod_to(u_ref, h_ref, nw, shift, scale):
    gain = nw * (1.0 + scale)
    for r in range(0, h_ref.shape[0], NORM_ROWS):
        h = h_ref[r:r + NORM_ROWS, :]
        var = jnp.mean(h * h, axis=-1, keepdims=True)
        u_ref[r:r + NORM_ROWS, :] = (h * lax.rsqrt(var + NORM_EPS) * gain + shift).astype(u_ref.dtype)


def _adaln_body(c_ref, w_ref, b_ref, o_ref):
    ca = _silu(c_ref[...]).astype(BF16)
    o_ref[...] = jnp.dot(ca, w_ref[...].astype(BF16), preferred_element_type=F32) + b_ref[...]


def _adaln(c, w, b):
    tn = ADALN_TN
    nb, d = c.shape
    n = w.shape[1]
    rows = -(-nb // SUBLANES) * SUBLANES
    c_pad = jnp.pad(c, ((0, rows - nb), (0, 0)))
    out = pl.pallas_call(
        _adaln_body,
        grid=(n // tn,),
        in_specs=[pl.BlockSpec((rows, d), lambda j: (0, 0)),
                  pl.BlockSpec((d, tn), lambda j: (0, j)),
                  pl.BlockSpec((1, tn), lambda j: (0, j))],
        out_specs=pl.BlockSpec((rows, tn), lambda j: (0, j)),
        out_shape=jax.ShapeDtypeStruct((rows, n), F32),
        compiler_params=_cparams(("arbitrary",)),
        name="adaln",
    )(c_pad, w, b.reshape(1, n))
    return out[:nb]


def _ffn_body(h_ref, nw_ref, sh_ref, sc_ref, g_ref, wg_ref, wu_ref, wd_ref, fn_ref,
              o_ref, u_scr, *, final_norm):
    j = pl.program_id(1)

    @pl.when(j == 0)
    def _():
        _norm_mod_to(u_scr, h_ref, nw_ref[...], sh_ref[0], sc_ref[0])
        o_ref[...] = jnp.zeros_like(o_ref)

    u = u_scr[...]
    gate = jnp.dot(u, wg_ref[...].astype(BF16), preferred_element_type=F32)
    up = jnp.dot(u, wu_ref[...].astype(BF16), preferred_element_type=F32)
    act = (_silu(gate) * up).astype(BF16)
    o_ref[...] += jnp.dot(act, wd_ref[...].astype(BF16), preferred_element_type=F32)

    @pl.when(j == pl.num_programs(1) - 1)
    def _():
        hn = h_ref[...] + 0.5 * g_ref[0] * o_ref[...]
        if final_norm:
            var = jnp.mean(hn * hn, axis=-1, keepdims=True)
            hn = hn * lax.rsqrt(var + NORM_EPS) * fn_ref[...]
        o_ref[...] = hn


def _ffn(h, mod3, seq, k0, nw, wg, wu, wd, fn, final_norm):
    tm, tf = FFN_TM, FFN_TF
    t, d = h.shape
    dff = wg.shape[1]
    mod_spec = lambda k: pl.BlockSpec((1, 1, d), lambda i, j: (((i * tm) // seq) * N_MOD + k, 0, 0))
    return pl.pallas_call(
        functools.partial(_ffn_body, final_norm=final_norm),
        grid=(t // tm, dff // tf),
        in_specs=[pl.BlockSpec((tm, d), lambda i, j: (i, 0)),
                  pl.BlockSpec((1, d), lambda i, j: (0, 0)),
                  mod_spec(k0), mod_spec(k0 + 1), mod_spec(k0 + 2),
                  pl.BlockSpec((d, tf), lambda i, j: (0, j)),
                  pl.BlockSpec((d, tf), lambda i, j: (0, j)),
                  pl.BlockSpec((tf, d), lambda i, j: (j, 0)),
                  pl.BlockSpec((1, d), lambda i, j: (0, 0))],
        out_specs=pl.BlockSpec((tm, d), lambda i, j: (i, 0)),
        out_shape=jax.ShapeDtypeStruct((t, d), F32),
        scratch_shapes=[pltpu.VMEM((tm, d), BF16)],
        compiler_params=_cparams(("arbitrary", "arbitrary"), FFN_VMEM_LIMIT),
        name="ffn_final" if final_norm else "ffn",
    )(h, nw.reshape(1, d), mod3, mod3, mod3, wg, wu, wd, fn.reshape(1, d))


def _mix_in_body(h_ref, nw_ref, sh_ref, sc_ref, w_ref, wab_ref, o_ref, ab_ref, u_scr):
    j = pl.program_id(1)

    @pl.when(j == 0)
    def _():
        _norm_mod_to(u_scr, h_ref, nw_ref[...], sh_ref[0], sc_ref[0])
        ab_ref[...] = jnp.dot(u_scr[...], wab_ref[...], preferred_element_type=F32)

    o_ref[...] = jnp.dot(u_scr[...], w_ref[...], preferred_element_type=F32).astype(o_ref.dtype)


def _mix_in(h, mod3, seq, nw, w_main, wab):
    tm, tn = MIX_IN_TM, MIX_IN_TN
    t, d = h.shape
    n = w_main.shape[1]
    mod_spec = lambda k: pl.BlockSpec((1, 1, d), lambda i, j: (((i * tm) // seq) * N_MOD + k, 0, 0))
    return pl.pallas_call(
        _mix_in_body,
        grid=(t // tm, n // tn),
        in_specs=[pl.BlockSpec((tm, d), lambda i, j: (i, 0)),
                  pl.BlockSpec((1, d), lambda i, j: (0, 0)),
                  mod_spec(3), mod_spec(4),
                  pl.BlockSpec((d, tn), lambda i, j: (0, j)),
                  pl.BlockSpec((d, LANES), lambda i, j: (0, 0))],
        out_specs=[pl.BlockSpec((tm, tn), lambda i, j: (i, j)),
                   pl.BlockSpec((tm, LANES), lambda i, j: (i, 0))],
        out_shape=[jax.ShapeDtypeStruct((t, n), BF16),
                   jax.ShapeDtypeStruct((t, LANES), F32)],
        scratch_shapes=[pltpu.VMEM((tm, d), BF16)],
        compiler_params=_cparams(("arbitrary", "arbitrary")),
        name="mix_in",
    )(h, nw.reshape(1, d), mod3, mod3, w_main, wab)


def _attn_body(sink_ref, pos_ref, invf_ref, rot_ref, q_ref, k_ref, v_ref, o_ref, kprev, vprev):
    n = pl.program_id(1)
    blk = ATTN_BLOCK
    hd = ATTN_HEAD_DIM
    pairs_per_kv = ATTN_HEADS // ATTN_KV_HEADS // 2
    n_pairs = ATTN_HEADS // 2

    @pl.when(n == 0)
    def _():
        kprev[...] = jnp.zeros_like(kprev)
        vprev[...] = jnp.zeros_like(vprev)

    ang = pos_ref[...].astype(F32) * invf_ref[...]
    cos = jnp.cos(ang)
    sin = jnp.sin(ang)
    rot = rot_ref[...]

    kin = k_ref[...]
    k = kin.astype(F32) * cos + jnp.dot(kin, rot, preferred_element_type=F32) * sin
    qin = jnp.concatenate([q_ref[:, p * LANES:(p + 1) * LANES] for p in range(n_pairs)], axis=0)
    cos_q = jnp.concatenate([cos] * n_pairs, axis=0)
    sin_q = jnp.concatenate([sin] * n_pairs, axis=0)
    q = (qin.astype(F32) * cos_q + jnp.dot(qin, rot, preferred_element_type=F32) * sin_q) * (hd ** -0.5)
    q = q.astype(BF16)

    lane2 = lax.broadcasted_iota(jnp.int32, (2 * blk, LANES), 1)
    left2 = jnp.where(lane2 < hd, 1.0, 0.0).astype(BF16)
    right2 = jnp.where(lane2 < hd, 0.0, 1.0).astype(BF16)
    v = v_ref[...]
    cur = [k.astype(BF16), pltpu.roll(k, hd, axis=1).astype(BF16)]
    cur_v = [v, pltpu.roll(v.astype(F32), hd, axis=1).astype(BF16)]
    kband = [jnp.concatenate([kprev[i], cur[i]], axis=0) for i in range(2)]
    vband = [jnp.concatenate([vprev[i], cur_v[i]], axis=0) for i in range(2)]
    ones_cols = jnp.concatenate([left2, right2], axis=0)

    qi = lax.broadcasted_iota(jnp.int32, (blk, blk), 0)
    kj = lax.broadcasted_iota(jnp.int32, (blk, blk), 1)
    upper = kj > qi
    left = kj < hd
    prev_bias = jnp.where(n > 0, 0.0, -jnp.inf)

    scores = []
    for g in range(ATTN_KV_HEADS):
        a, b = (0, 1) if g == 0 else (1, 0)
        kbd = jnp.concatenate([kband[a] * left2, kband[b] * right2], axis=0)
        qg = q[g * pairs_per_kv * blk:(g + 1) * pairs_per_kv * blk]
        scores.append(lax.dot_general(qg, kbd, (((1,), (1,)), ((), ())), preferred_element_type=F32))

    probs, mx = [], []
    for p in range(n_pairs):
        g, pp = divmod(p, pairs_per_kv)
        sp = scores[g][pp * blk:(pp + 1) * blk]
        tiles, ms = [], []
        for i in range(2):
            comb = jnp.where(upper, sp[:, 2 * i * blk:(2 * i + 1) * blk] + prev_bias,
                             sp[:, (2 * i + 1) * blk:(2 * i + 2) * blk])
            m = jnp.maximum(jnp.max(comb, axis=-1, keepdims=True), sink_ref[2 * p + i])
            e = jnp.exp(comb - m)
            tiles += [jnp.where(upper, e, 0.0).astype(BF16), jnp.where(upper, 0.0, e).astype(BF16)]
            ms.append(m)
        probs.append(jnp.concatenate(tiles, axis=1))
        mx.append(ms)

    for g in range(ATTN_KV_HEADS):
        a, b = (0, 1) if g == 0 else (1, 0)
        vbd = jnp.concatenate([vband[a] * left2, vband[b] * right2], axis=0)
        vext = jnp.concatenate([vbd, ones_cols], axis=1)
        pg = jnp.concatenate(probs[g * pairs_per_kv:(g + 1) * pairs_per_kv], axis=0)
        res = jnp.dot(pg, vext, preferred_element_type=F32)
        for pp in range(pairs_per_kv):
            p = g * pairs_per_kv + pp
            rp = res[pp * blk:(pp + 1) * blk]
            sink_mass = jnp.where(left, jnp.exp(sink_ref[2 * p] - mx[p][0]),
                                  jnp.exp(sink_ref[2 * p + 1] - mx[p][1]))
            o = rp[:, :LANES] / (rp[:, LANES:] + sink_mass)
            o_ref[:, p * LANES:(p + 1) * LANES] = o.astype(o_ref.dtype)

    for i in range(2):
        kprev[i] = cur[i]
        vprev[i] = cur_v[i]


def _attention(proj, positions, sinks, batch, seq):
    assert WINDOW == ATTN_BLOCK and 2 * ATTN_HEAD_DIM == LANES
    t = batch * seq
    nb = seq // ATTN_BLOCK
    blk = ATTN_BLOCK
    half = ROPE_DIM // 2
    inv_freq = ROPE_THETA ** (-jnp.arange(0, ROPE_DIM, 2, dtype=F32) / ROPE_DIM)
    head_pat = jnp.concatenate([inv_freq, inv_freq, jnp.zeros((ATTN_HEAD_DIM - 2 * half,), F32)])
    invf = jnp.tile(head_pat, LANES // ATTN_HEAD_DIM).reshape(1, LANES)
    rot = np.zeros((LANES, LANES), np.float32)
    for base in range(0, LANES, ATTN_HEAD_DIM):
        for d in range(half):
            rot[base + d + half, base + d] = -1.0
            rot[base + d, base + d + half] = 1.0
    kcol = COL_ATTN_K // LANES
    row = lambda b, n: b * nb + n
    return pl.pallas_call(
        _attn_body,
        grid=(batch, nb),
        in_specs=[pl.BlockSpec(memory_space=pltpu.SMEM),
                  pl.BlockSpec((blk, 1), lambda b, n: (row(b, n), 0)),
                  pl.BlockSpec((1, LANES), lambda b, n: (0, 0)),
                  pl.BlockSpec((LANES, LANES), lambda b, n: (0, 0)),
                  pl.BlockSpec((blk, ATTN_WIDTH), lambda b, n: (row(b, n), COL_ATTN_Q // ATTN_WIDTH)),
                  pl.BlockSpec((blk, LANES), lambda b, n: (row(b, n), kcol)),
                  pl.BlockSpec((blk, LANES), lambda b, n: (row(b, n), kcol + 1))],
        out_specs=pl.BlockSpec((blk, ATTN_WIDTH), lambda b, n: (row(b, n), 0)),
        out_shape=jax.ShapeDtypeStruct((t, ATTN_WIDTH), BF16),
        scratch_shapes=[pltpu.VMEM((2, blk, LANES), BF16), pltpu.VMEM((2, blk, LANES), BF16)],
        compiler_params=_cparams(("arbitrary", "arbitrary")),
        name="attn",
    )(sinks, positions.reshape(t, 1), invf, jnp.asarray(rot, BF16), proj, proj, proj)


INV_BASE = 8
DN_STEP_CHUNKS = 2


def _split_bf16(t):
    hi = t.astype(BF16)
    lo = (t - hi.astype(F32)).astype(BF16)
    return hi, lo


def _split3_bf16(t):
    hi = t.astype(BF16)
    r1 = t - hi.astype(F32)
    mid = r1.astype(BF16)
    lo = (r1 - mid.astype(F32)).astype(BF16)
    return hi, mid, lo


def _pair_blockdiag(t, halves):
    return jnp.concatenate([t * halves[0], t * halves[1]], axis=0)


def _pair_dot(a_terms, b_terms, halves):
    m = a_terms[0].shape[0]
    out = None
    for j, b in enumerate(b_terms):
        lhs = a_terms if j == 0 else a_terms[:1]
        prod = jnp.dot(jnp.concatenate(lhs, axis=0), _pair_blockdiag(b, halves),
                       preferred_element_type=F32)
        for i in range(len(lhs)):
            part = prod[i * m:(i + 1) * m]
            out = part if out is None else out + part
    return out


def _inverse_masks(r, s, c):
    same = lambda bs: (r // bs) == (s // bs)
    masks = [same(INV_BASE)]
    bs = INV_BASE
    while bs < c:
        masks.append(same(2 * bs) & jnp.logical_not(same(bs)))
        bs *= 2
    return [jnp.where(m, 1.0, 0.0).astype(BF16) for m in masks]


def _unit_lower_inverse(a_list, eye, masks, halves):
    c = a_list[0].shape[0]
    dot = lambda x, y: _pair_dot(x, y, halves)
    split = _split_bf16
    d = [a * masks[0] for a in a_list]
    p = [eye - di.astype(F32) for di in d]
    x = [dot((di,), (di,)) for di in d]
    both = [dot(split(jnp.concatenate([xi, pi], axis=0)), split(xi))
            for xi, pi in zip(x, p)]
    p = [pi + bi[c:] for pi, bi in zip(p, both)]
    p = [pi + dot(split(pi), split(bi[:c])) for pi, bi in zip(p, both)]
    for m in masks[1:]:
        ps = [split(pi) for pi in p]
        pm = [dot(psi, (a * m,)) for psi, a in zip(ps, a_list)]
        p = [pi - dot(split(pmi), psi) for pi, pmi, psi in zip(p, pm, ps)]
    return p


def _softplus(t):
    return jnp.maximum(t, 0.0) + jnp.log(1.0 + jnp.exp(-jnp.abs(t)))


def _cumsum_rows(lower, t):
    n = t.shape[1]
    parts = jnp.dot(lower.astype(BF16), jnp.concatenate(_split3_bf16(t), axis=1),
                    preferred_element_type=F32)
    return parts[:, :n] + parts[:, n:2 * n] + parts[:, 2 * n:]


def _cumsum_lanes(t, upper):
    m = t.shape[0]
    parts = jnp.dot(jnp.concatenate(_split3_bf16(t), axis=0), upper.astype(BF16),
                    preferred_element_type=F32)
    return parts[:m] + parts[m:2 * m] + parts[2 * m:]


def _dn_body(qkv_ref, gate_ref, ab_ref, shift_ref, cw_ref, alog_ref, dtb_ref,
             alogt_ref, dtbt_ref, nw_ref, o_ref, zbuf, state):
    ci = pl.program_id(1)
    c = DN_CHUNK
    nch = DN_STEP_CHUNKS
    rows = nch * c
    hk = DN_HEAD_K
    dv = DN_HEAD_V
    width = DN_WIDTH
    pad = BF16_ROWS
    hist = CONV_K - 1
    heads = range(DN_HEADS)
    pairs = range(DN_HEADS // 2)
    chunks = range(nch)

    @pl.when(ci == 0)
    def _():
        state[...] = jnp.zeros_like(state)
        zbuf[0:pad, :] = jnp.zeros((pad, 3 * width), BF16)

    zbuf[pad:pad + rows, :] = qkv_ref[...]
    shifted = jnp.dot(shift_ref[...], zbuf[...], preferred_element_type=F32)
    y = qkv_ref[...].astype(F32) * cw_ref[hist:hist + 1, :]
    for j in range(hist):
        y = y + shifted[j * rows:(j + 1) * rows] * cw_ref[j:j + 1, :]
    zbuf[0:pad, :] = zbuf[rows:rows + pad, :]
    y = _silu(y)

    n_qk = 2 * DN_HEADS
    sq = y[:, :2 * width] * y[:, :2 * width]
    sq = jnp.concatenate([sq[:, i * hk:(i + 1) * hk] for i in range(n_qk)], axis=0)
    sums = jnp.dot(jnp.concatenate(_split_bf16(sq), axis=0), jnp.ones((hk, hk), BF16),
                   preferred_element_type=F32)
    rnorm = lax.rsqrt(sums[:n_qk * rows] + sums[n_qk * rows:] + L2_EPS)

    r = lax.broadcasted_iota(jnp.int32, (c, 2 * c), 0)
    lane = lax.broadcasted_iota(jnp.int32, (c, 2 * c), 1)
    left = lane < c
    halves = (jnp.where(left, 1.0, 0.0).astype(BF16), jnp.where(left, 0.0, 1.0).astype(BF16))
    s = jnp.where(left, lane, lane - c)
    tri = r >= s
    strict = r > s
    eye = jnp.where(r == s, 1.0, 0.0).astype(F32)
    inv_masks = _inverse_masks(r, s, c)

    ab = ab_ref[...]
    g_col_raw = -jnp.exp(alog_ref[...]) * _softplus(ab + dtb_ref[...])
    beta_col = jax.nn.sigmoid(ab)
    rr = lax.broadcasted_iota(jnp.int32, (rows, rows), 0)
    ss = lax.broadcasted_iota(jnp.int32, (rows, rows), 1)
    lower = jnp.where((rr >= ss) & ((rr // c) == (ss // c)), 1.0, 0.0).astype(F32)
    g_col = _cumsum_rows(lower, g_col_raw)
    a_rows = ab.T[0:DN_HEADS]
    g_row_raw = -jnp.exp(alogt_ref[...]) * _softplus(a_rows + dtbt_ref[...])
    upper = jnp.where((rr <= ss) & ((rr // c) == (ss // c)), 1.0, 0.0).astype(F32)
    g_row = _cumsum_lanes(g_row_raw, upper)
    g_row_sw = pltpu.roll(g_row, c, axis=1)

    q, k, kb, vb, kbg, qg = [], [], [], [], [], []
    for h in heads:
        qh = y[:, h * hk:(h + 1) * hk]
        kh = y[:, width + h * hk:width + (h + 1) * hk]
        vh = y[:, 2 * width + h * dv:2 * width + (h + 1) * dv]
        qh = qh * rnorm[h * rows:(h + 1) * rows] * (hk ** -0.5)
        kh = kh * rnorm[(DN_HEADS + h) * rows:(DN_HEADS + h + 1) * rows]
        beta = beta_col[:, DN_HEADS + h:DN_HEADS + h + 1]
        eg = jnp.exp(g_col[:, h:h + 1])
        q.append(qh)
        k.append(kh)
        kb.append(kh * beta)
        vb.append(vh * beta)
        kbg.append(kb[h] * eg)
        qg.append(qh * eg)

    a_list, ai_list = [], []
    zeros_k = jnp.zeros((c, hk), F32)
    for cc in chunks:
        cs = slice(cc * c, (cc + 1) * c)
        for j in pairs:
            h0, h1 = 2 * j, 2 * j + 1
            gcp = jnp.where(left, jnp.broadcast_to(g_col[cs, h0:h0 + 1], (c, 2 * c)),
                            jnp.broadcast_to(g_col[cs, h1:h1 + 1], (c, 2 * c)))
            grp = (jnp.where(left[0:1], g_row[h0:h0 + 1], g_row_sw[h1:h1 + 1]) if cc == 0 else
                   jnp.where(left[0:1], g_row_sw[h0:h0 + 1], g_row[h1:h1 + 1]))
            decay = jnp.exp(jnp.where(tri, gcp - grp, -jnp.inf))
            lhs = jnp.concatenate([jnp.concatenate([kb[h0][cs], kb[h1][cs]], axis=1),
                                   jnp.concatenate([q[h0][cs], q[h1][cs]], axis=1)], axis=0)
            kbd = jnp.concatenate([jnp.concatenate([k[h0][cs], zeros_k], axis=1),
                                   jnp.concatenate([zeros_k, k[h1][cs]], axis=1)], axis=0)
            kq = lax.dot_general(lhs.astype(BF16), kbd.astype(BF16), (((1,), (1,)), ((), ())),
                                 preferred_element_type=F32)
            a_list.append(jnp.where(strict, kq[:c] * decay, 0.0).astype(BF16))
            ai_list.append(jnp.where(tri, kq[c:] * decay, 0.0).astype(BF16))

    t_list = _unit_lower_inverse(a_list, eye, inv_masks, halves)

    u = [[None] * DN_HEADS for _ in chunks]
    w = [[None] * DN_HEADS for _ in chunks]
    zeros_v = jnp.zeros((c, dv + hk), F32)
    for cc in chunks:
        cs = slice(cc * c, (cc + 1) * c)
        for j in pairs:
            h0, h1 = 2 * j, 2 * j + 1
            rhs = jnp.concatenate([jnp.concatenate([vb[h0][cs], kbg[h0][cs], zeros_v], axis=1),
                                   jnp.concatenate([zeros_v, vb[h1][cs], kbg[h1][cs]], axis=1)], axis=0)
            uw = jnp.dot(t_list[cc * len(pairs) + j].astype(BF16), rhs.astype(BF16),
                         preferred_element_type=F32)
            u[cc][h0], w[cc][h0] = uw[:, :dv], uw[:, dv:dv + hk]
            u[cc][h1], w[cc][h1] = uw[:, dv + hk:2 * dv + hk], uw[:, 2 * dv + hk:]

    st = [state[h] for h in heads]
    zeros_b = jnp.zeros((c, dv), BF16)
    for cc in chunks:
        cs = slice(cc * c, (cc + 1) * c)
        ws = [jnp.dot(jnp.concatenate([w[cc][h], qg[h][cs]], axis=0).astype(BF16), st[h].astype(BF16),
                      preferred_element_type=F32) for h in heads]
        v_new = [(u[cc][h] - ws[h][:c]).astype(BF16) for h in heads]
        intra = []
        for j in pairs:
            h0, h1 = 2 * j, 2 * j + 1
            vbd = jnp.concatenate([jnp.concatenate([v_new[h0], zeros_b], axis=1),
                                   jnp.concatenate([zeros_b, v_new[h1]], axis=1)], axis=0)
            intra.append(jnp.dot(ai_list[cc * len(pairs) + j], vbd, preferred_element_type=F32))
        for h in heads:
            j, half = divmod(h, 2)
            o = ws[h][c:] + intra[j][:, half * dv:(half + 1) * dv]
            gc = g_col[cs, h:h + 1]
            gl = gc[c - 1:c, :]
            kd = k[h][cs] * jnp.exp(gl - gc)
            st[h] = st[h] * jnp.exp(gl) + lax.dot_general(
                kd.astype(BF16), v_new[h], (((0,), (0,)), ((), ())), preferred_element_type=F32)
            var = jnp.mean(o * o, axis=-1, keepdims=True)
            on = o * lax.rsqrt(var + NORM_EPS) * nw_ref[...]
            hs = slice(h * dv, (h + 1) * dv)
            o_ref[cs, hs] = (on * _silu(gate_ref[cs, hs].astype(F32))).astype(o_ref.dtype)
    for h in heads:
        state[h] = st[h]


def _deltanet(proj, ab, conv_w, a_log, dt_bias, norm_w, batch, seq):
    t = batch * seq
    c = DN_CHUNK
    nch = DN_STEP_CHUNKS
    rows = nch * c
    ns = seq // rows
    width = DN_WIDTH
    row = lambda b, i: b * ns + i
    assert rows == LANES
    lane_pad = lambda p: jnp.pad(p.reshape(1, -1), ((0, 0), (0, LANES - p.shape[0])))
    const = lambda shape: pl.BlockSpec(shape, lambda b, i: (0,) * len(shape))
    hist = CONV_K - 1
    shift = np.zeros((hist * rows, BF16_ROWS + rows), np.float32)
    for j in range(hist):
        shift[j * rows + np.arange(rows), BF16_ROWS + np.arange(rows) - hist + j] = 1.0
    return pl.pallas_call(
        _dn_body,
        grid=(batch, ns),
        in_specs=[pl.BlockSpec((rows, 3 * width), lambda b, i: (row(b, i), COL_DN_QKV // (3 * width))),
                  pl.BlockSpec((rows, width), lambda b, i: (row(b, i), COL_DN_GATE // width)),
                  pl.BlockSpec((rows, LANES), lambda b, i: (row(b, i), 0)),
                  const((hist * rows, BF16_ROWS + rows)),
                  const((CONV_K, 3 * width)),
                  const((1, LANES)), const((1, LANES)),
                  const((DN_HEADS, 1)), const((DN_HEADS, 1)),
                  const((1, DN_HEAD_V))],
        out_specs=pl.BlockSpec((rows, width), lambda b, i: (row(b, i), 0)),
        out_shape=jax.ShapeDtypeStruct((t, width), BF16),
        scratch_shapes=[pltpu.VMEM((BF16_ROWS + rows, 3 * width), BF16),
                        pltpu.VMEM((DN_HEADS, DN_HEAD_K, DN_HEAD_V), F32)],
        compiler_params=_cparams(("arbitrary", "arbitrary")),
        name="dn",
    )(proj, proj, ab, jnp.asarray(shift, BF16), conv_w, lane_pad(a_log), lane_pad(dt_bias),
      a_log.reshape(-1, 1), dt_bias.reshape(-1, 1), norm_w.reshape(1, -1))


def _mix_out_body(h_ref, g_ref, a_ref, d_ref, wa_ref, wd_ref, o_ref):
    mixed = jnp.dot(a_ref[...], wa_ref[...], preferred_element_type=F32)
    mixed = mixed + jnp.dot(d_ref[...], wd_ref[...], preferred_element_type=F32)
    o_ref[...] = h_ref[...] + g_ref[0] * mixed


def _mix_out(h, mod3, seq, attn_out, dn_out, w_out):
    tm = TOKEN_TILE
    t, d = h.shape
    assert ATTN_WIDTH == DN_WIDTH
    return pl.pallas_call(
        _mix_out_body,
        grid=(t // tm,),
        in_specs=[pl.BlockSpec((tm, d), lambda i: (i, 0)),
                  pl.BlockSpec((1, 1, d), lambda i: (((i * tm) // seq) * N_MOD + 5, 0, 0)),
                  pl.BlockSpec((tm, ATTN_WIDTH), lambda i: (i, 0)),
                  pl.BlockSpec((tm, DN_WIDTH), lambda i: (i, 0)),
                  pl.BlockSpec((ATTN_WIDTH, d), lambda i: (0, 0)),
                  pl.BlockSpec((DN_WIDTH, d), lambda i: (1, 0))],
        out_specs=pl.BlockSpec((tm, d), lambda i: (i, 0)),
        out_shape=jax.ShapeDtypeStruct((t, d), F32),
        compiler_params=_cparams(("arbitrary",)),
        name="mix_out",
    )(h, mod3, attn_out, dn_out, w_out, w_out)


def kernel(x, c, positions, ada_w, ada_b, norm_ffn1, ffn1_w_gate, ffn1_w_up, ffn1_w_down, norm_mix, w_in, conv_w, a_log, dt_bias, attn_sinks, dn_norm_w, w_out, norm_ffn2, ffn2_w_gate, ffn2_w_up, ffn2_w_down, final_norm):
    batch, seq, d = x.shape
    depth = ada_w.shape[0]
    assert depth >= 1 and seq % ATTN_BLOCK == 0 and seq % TOKEN_TILE == 0
    assert seq % (DN_CHUNK * DN_STEP_CHUNKS) == 0
    t = batch * seq
    h = x.reshape(t, d)
    assert seq % FFN_TM == 0 and seq % MIX_IN_TM == 0 and COL_DN_AB % MIX_IN_TN == 0
    o_dq = ATTN_WIDTH + 2 * ATTN_KV_WIDTH
    o_ab = o_dq + 4 * DN_WIDTH
    for l in range(depth):
        mod3 = _adaln(c, ada_w[l], ada_b[l]).reshape(batch * N_MOD, 1, d)
        h = _ffn(h, mod3, seq, 0, norm_ffn1[l], ffn1_w_gate[l], ffn1_w_up[l], ffn1_w_down[l],
                 final_norm, False)
        w = w_in[l]
        w_main = jnp.concatenate([w[:, o_dq:o_ab].astype(BF16), w[:, :o_dq].astype(BF16)], axis=1)
        w_ab = jnp.pad(w[:, o_ab:].astype(BF16), ((0, 0), (0, LANES - 2 * DN_HEADS)))
        proj, ab = _mix_in(h, mod3, seq, norm_mix[l], w_main, w_ab)
        attn_out = _attention(proj, positions, attn_sinks[l], batch, seq)
        dn_out = _deltanet(proj, ab, conv_w[l], a_log[l], dt_bias[l], dn_norm_w[l], batch, seq)
        h = _mix_out(h, mod3, seq, attn_out, dn_out, w_out[l].astype(BF16))
        h = _ffn(h, mod3, seq, 6, norm_ffn2[l], ffn2_w_gate[l], ffn2_w_up[l], ffn2_w_down[l],
                 final_norm, l == depth - 1)
    return h.reshape(batch, seq, d)
```

```python
import functools

import numpy as np
import jax
import jax.numpy as jnp
from jax import lax
from jax.experimental import pallas as pl
from jax.experimental.pallas import tpu as pltpu

F32 = jnp.float32
BF16 = jnp.bfloat16

ATTN_HEADS = 16
ATTN_KV_HEADS = 2
ATTN_HEAD_DIM = 64
WINDOW = 128
ATTN_BLOCK = 128
ROPE_THETA = 500000.0
ROPE_DIM = ATTN_HEAD_DIM // 4
DN_HEADS = 8
DN_HEAD_K = 128
DN_HEAD_V = 128
DN_CHUNK = 64
CONV_K = 4
NORM_EPS = 1e-6
L2_EPS = 1e-6
N_MOD = 9

ATTN_WIDTH = ATTN_HEADS * ATTN_HEAD_DIM
ATTN_KV_WIDTH = ATTN_KV_HEADS * ATTN_HEAD_DIM
DN_WIDTH = DN_HEADS * DN_HEAD_V
IN_WIDTH = ATTN_WIDTH + 2 * ATTN_KV_WIDTH + 4 * DN_WIDTH + 2 * DN_HEADS
COL_DN_QKV = 0
COL_DN_GATE = 3 * DN_WIDTH
COL_ATTN_Q = 4 * DN_WIDTH
COL_ATTN_K = COL_ATTN_Q + ATTN_WIDTH
COL_ATTN_V = COL_ATTN_K + ATTN_KV_WIDTH
COL_DN_AB = COL_ATTN_V + ATTN_KV_WIDTH
LANES = 128
SUBLANES = 8
BF16_ROWS = 16
VMEM_LIMIT = 56 * 1024 * 1024
FFN_VMEM_LIMIT = 60 * 1024 * 1024
TOKEN_TILE = 512
FFN_TM = 1024
FFN_TF = 256
MIX_IN_TM = 1024
MIX_IN_TN = 1792
ADALN_TN = 1024
NORM_ROWS = 128


def _cparams(sem, vmem_limit=VMEM_LIMIT):
    return pltpu.CompilerParams(dimension_semantics=sem, vmem_limit_bytes=vmem_limit)


def _silu(t):
    return t * jax.nn.sigmoid(t)


def _norm_mod_to(u_ref, h_ref, nw, shift, scale):
    gain = nw * (1.0 + scale)
    for r in range(0, h_ref.shape[0], NORM_ROWS):
        h = h_ref[r:r + NORM_ROWS, :]
        var = jnp.mean(h * h, axis=-1, keepdims=True)
        u_ref[r:r + NORM_ROWS, :] = (h * lax.rsqrt(var + NORM_EPS) * gain + shift).astype(u_ref.dtype)


def _adaln_body(c_ref, w_ref, b_ref, o_ref):
    ca = _silu(c_ref[...]).astype(BF16)
    o_ref[...] = jnp.dot(ca, w_ref[...].astype(BF16), preferred_element_type=F32) + b_ref[...]


def _adaln(c, w, b):
    tn = ADALN_TN
    nb, d = c.shape
    n = w.shape[1]
    rows = -(-nb // SUBLANES) * SUBLANES
    c_pad = jnp.pad(c, ((0, rows - nb), (0, 0)))
    out = pl.pallas_call(
        _adaln_body,
        grid=(n // tn,),
        in_specs=[pl.BlockSpec((rows, d), lambda j: (0, 0)),
                  pl.BlockSpec((d, tn), lambda j: (0, j)),
                  pl.BlockSpec((1, tn), lambda j: (0, j))],
        out_specs=pl.BlockSpec((rows, tn), lambda j: (0, j)),
        out_shape=jax.ShapeDtypeStruct((rows, n), F32),
        compiler_params=_cparams(("arbitrary",)),
        name="adaln",
    )(c_pad, w, b.reshape(1, n))
    return out[:nb]


def _ffn_body(h_ref, nw_ref, sh_ref, sc_ref, g_ref, wg_ref, wu_ref, wd_ref, fn_ref,
              o_ref, u_scr, *, final_norm):
    j = pl.program_id(1)

    def down_proj():
        u = u_scr[...]
        gate = jnp.dot(u, wg_ref[...].astype(BF16), preferred_element_type=F32)
        up = jnp.dot(u, wu_ref[...].astype(BF16), preferred_element_type=F32)
        act = (_silu(gate) * up).astype(BF16)
        return jnp.dot(act, wd_ref[...].astype(BF16), preferred_element_type=F32)

    @pl.when(j == 0)
    def _():
        _norm_mod_to(u_scr, h_ref, nw_ref[...], sh_ref[0], sc_ref[0])
        o_ref[...] = down_proj()

    @pl.when(j > 0)
    def _():
        o_ref[...] += down_proj()

    @pl.when(j == pl.num_programs(1) - 1)
    def _():
        hn = h_ref[...] + 0.5 * g_ref[0] * o_ref[...]
        if final_norm:
            var = jnp.mean(hn * hn, axis=-1, keepdims=True)
            hn = hn * lax.rsqrt(var + NORM_EPS) * fn_ref[...]
        o_ref[...] = hn


def _ffn(h, mod3, seq, k0, nw, wg, wu, wd, fn, final_norm):
    tm, tf = FFN_TM, FFN_TF
    t, d = h.shape
    dff = wg.shape[1]
    mod_spec = lambda k: pl.BlockSpec((1, 1, d), lambda i, j: (((i * tm) // seq) * N_MOD + k, 0, 0))
    return pl.pallas_call(
        functools.partial(_ffn_body, final_norm=final_norm),
        grid=(t // tm, dff // tf),
        in_specs=[pl.BlockSpec((tm, d), lambda i, j: (i, 0)),
                  pl.BlockSpec((1, d), lambda i, j: (0, 0)),
                  mod_spec(k0), mod_spec(k0 + 1), mod_spec(k0 + 2),
                  pl.BlockSpec((d, tf), lambda i, j: (0, j)),
                  pl.BlockSpec((d, tf), lambda i, j: (0, j)),
                  pl.BlockSpec((tf, d), lambda i, j: (j, 0)),
                  pl.BlockSpec((1, d), lambda i, j: (0, 0))],
        out_specs=pl.BlockSpec((tm, d), lambda i, j: (i, 0)),
        out_shape=jax.ShapeDtypeStruct((t, d), F32),
        scratch_shapes=[pltpu.VMEM((tm, d), BF16)],
        compiler_params=_cparams(("arbitrary", "arbitrary"), FFN_VMEM_LIMIT),
        name="ffn_final" if final_norm else "ffn",
    )(h, nw.reshape(1, d), mod3, mod3, mod3, wg, wu, wd, fn.reshape(1, d))


def _mix_in_body(h_ref, nw_ref, sh_ref, sc_ref, w_ref, wab_ref, o_ref, ab_ref, u_scr):
    j = pl.program_id(1)

    def project():
        o_ref[...] = jnp.dot(u_scr[...], w_ref[...], preferred_element_type=F32).astype(o_ref.dtype)

    @pl.when(j == 0)
    def _():
        _norm_mod_to(u_scr, h_ref, nw_ref[...], sh_ref[0], sc_ref[0])
        ab_ref[...] = jnp.dot(u_scr[...], wab_ref[...], preferred_element_type=F32)
        project()

    @pl.when(j > 0)
    def _():
        project()


def _mix_in(h, mod3, seq, nw, w_main, wab):
    tm, tn = MIX_IN_TM, MIX_IN_TN
    t, d = h.shape
    n = w_main.shape[1]
    mod_spec = lambda k: pl.BlockSpec((1, 1, d), lambda i, j: (((i * tm) // seq) * N_MOD + k, 0, 0))
    return pl.pallas_call(
        _mix_in_body,
        grid=(t // tm, n // tn),
        in_specs=[pl.BlockSpec((tm, d), lambda i, j: (i, 0)),
                  pl.BlockSpec((1, d), lambda i, j: (0, 0)),
                  mod_spec(3), mod_spec(4),
                  pl.BlockSpec((d, tn), lambda i, j: (0, j)),
                  pl.BlockSpec((d, LANES), lambda i, j: (0, 0))],
        out_specs=[pl.BlockSpec((tm, tn), lambda i, j: (i, j)),
                   pl.BlockSpec((tm, LANES), lambda i, j: (i, 0))],
        out_shape=[jax.ShapeDtypeStruct((t, n), BF16),
                   jax.ShapeDtypeStruct((t, LANES), F32)],
        scratch_shapes=[pltpu.VMEM((tm, d), BF16)],
        compiler_params=_cparams(("arbitrary", "arbitrary")),
        name="mix_in",
    )(h, nw.reshape(1, d), mod3, mod3, w_main, wab)


def _attn_body(sink_ref, pos_ref, invf_ref, rot_ref, q_ref, k_ref, v_ref, o_ref, kprev, vprev):
    n = pl.program_id(1)
    blk = ATTN_BLOCK
    hd = ATTN_HEAD_DIM
    pairs_per_kv = ATTN_HEADS // ATTN_KV_HEADS // 2
    n_pairs = ATTN_HEADS // 2

    @pl.when(n == 0)
    def _():
        kprev[...] = jnp.zeros_like(kprev)
        vprev[...] = jnp.zeros_like(vprev)

    ang = pos_ref[...].astype(F32) * invf_ref[...]
    cos = jnp.cos(ang)
    sin = jnp.sin(ang)
    rot = rot_ref[...]

    kin = k_ref[...]
    k = kin.astype(F32) * cos + jnp.dot(kin, rot, preferred_element_type=F32) * sin
    qin = jnp.concatenate([q_ref[:, p * LANES:(p + 1) * LANES] for p in range(n_pairs)], axis=0)
    cos_q = jnp.concatenate([cos] * n_pairs, axis=0)
    sin_q = jnp.concatenate([sin] * n_pairs, axis=0)
    q = (qin.astype(F32) * cos_q + jnp.dot(qin, rot, preferred_element_type=F32) * sin_q) * (hd ** -0.5)
    q = q.astype(BF16)

    lane2 = lax.broadcasted_iota(jnp.int32, (2 * blk, LANES), 1)
    left2 = jnp.where(lane2 < hd, 1.0, 0.0).astype(BF16)
    right2 = jnp.where(lane2 < hd, 0.0, 1.0).astype(BF16)
    v = v_ref[...]
    cur = [k.astype(BF16), pltpu.roll(k, hd, axis=1).astype(BF16)]
    cur_v = [v, pltpu.roll(v.astype(F32), hd, axis=1).astype(BF16)]
    kband = [jnp.concatenate([kprev[i], cur[i]], axis=0) for i in range(2)]
    vband = [jnp.concatenate([vprev[i], cur_v[i]], axis=0) for i in range(2)]
    ones_cols = jnp.concatenate([left2, right2], axis=0)

    qi = lax.broadcasted_iota(jnp.int32, (blk, blk), 0)
    kj = lax.broadcasted_iota(jnp.int32, (blk, blk), 1)
    upper = kj > qi
    left = kj < hd
    prev_bias = jnp.where(n > 0, 0.0, -jnp.inf)

    scores = []
    for g in range(ATTN_KV_HEADS):
        a, b = (0, 1) if g == 0 else (1, 0)
        kbd = jnp.concatenate([kband[a] * left2, kband[b] * right2], axis=0)
        qg = q[g * pairs_per_kv * blk:(g + 1) * pairs_per_kv * blk]
        scores.append(lax.dot_general(qg, kbd, (((1,), (1,)), ((), ())), preferred_element_type=F32))

    probs, mx = [], []
    for p in range(n_pairs):
        g, pp = divmod(p, pairs_per_kv)
        sp = scores[g][pp * blk:(pp + 1) * blk]
        tiles, ms = [], []
        for i in range(2):
            comb = jnp.where(upper, sp[:, 2 * i * blk:(2 * i + 1) * blk] + prev_bias,
                             sp[:, (2 * i + 1) * blk:(2 * i + 2) * blk])
            m = jnp.maximum(jnp.max(comb, axis=-1, keepdims=True), sink_ref[2 * p + i])
            e = jnp.exp(comb - m)
            tiles += [jnp.where(upper, e, 0.0).astype(BF16), jnp.where(upper, 0.0, e).astype(BF16)]
            ms.append(m)
        probs.append(jnp.concatenate(tiles, axis=1))
        mx.append(ms)

    for g in range(ATTN_KV_HEADS):
        a, b = (0, 1) if g == 0 else (1, 0)
        vbd = jnp.concatenate([vband[a] * left2, vband[b] * right2], axis=0)
        vext = jnp.concatenate([vbd, ones_cols], axis=1)
        pg = jnp.concatenate(probs[g * pairs_per_kv:(g + 1) * pairs_per_kv], axis=0)
        res = jnp.dot(pg, vext, preferred_element_type=F32)
        for pp in range(pairs_per_kv):
            p = g * pairs_per_kv + pp
            rp = res[pp * blk:(pp + 1) * blk]
            sink_mass = jnp.where(left, jnp.exp(sink_ref[2 * p] - mx[p][0]),
                                  jnp.exp(sink_ref[2 * p + 1] - mx[p][1]))
            o = rp[:, :LANES] / (rp[:, LANES:] + sink_mass)
            o_ref[:, p * LANES:(p + 1) * LANES] = o.astype(o_ref.dtype)

    for i in range(2):
        kprev[i] = cur[i]
        vprev[i] = cur_v[i]


def _attention(proj, positions, sinks, batch, seq):
    assert WINDOW == ATTN_BLOCK and 2 * ATTN_HEAD_DIM == LANES
    t = batch * seq
    nb = seq // ATTN_BLOCK
    blk = ATTN_BLOCK
    half = ROPE_DIM // 2
    inv_freq = ROPE_THETA ** (-jnp.arange(0, ROPE_DIM, 2, dtype=F32) / ROPE_DIM)
    head_pat = jnp.concatenate([inv_freq, inv_freq, jnp.zeros((ATTN_HEAD_DIM - 2 * half,), F32)])
    invf = jnp.tile(head_pat, LANES // ATTN_HEAD_DIM).reshape(1, LANES)
    rot = np.zeros((LANES, LANES), np.float32)
    for base in range(0, LANES, ATTN_HEAD_DIM):
        for d in range(half):
            rot[base + d + half, base + d] = -1.0
            rot[base + d, base + d + half] = 1.0
    kcol = COL_ATTN_K // LANES
    row = lambda b, n: b * nb + n
    return pl.pallas_call(
        _attn_body,
        grid=(batch, nb),
        in_specs=[pl.BlockSpec(memory_space=pltpu.SMEM),
                  pl.BlockSpec((blk, 1), lambda b, n: (row(b, n), 0)),
                  pl.BlockSpec((1, LANES), lambda b, n: (0, 0)),
                  pl.BlockSpec((LANES, LANES), lambda b, n: (0, 0)),
                  pl.BlockSpec((blk, ATTN_WIDTH), lambda b, n: (row(b, n), COL_ATTN_Q // ATTN_WIDTH)),
                  pl.BlockSpec((blk, LANES), lambda b, n: (row(b, n), kcol)),
                  pl.BlockSpec((blk, LANES), lambda b, n: (row(b, n), kcol + 1))],
        out_specs=pl.BlockSpec((blk, ATTN_WIDTH), lambda b, n: (row(b, n), 0)),
        out_shape=jax.ShapeDtypeStruct((t, ATTN_WIDTH), BF16),
        scratch_shapes=[pltpu.VMEM((2, blk, LANES), BF16), pltpu.VMEM((2, blk, LANES), BF16)],
        compiler_params=_cparams(("arbitrary", "arbitrary")),
        name="attn",
    )(sinks, positions.reshape(t, 1), invf, jnp.asarray(rot, BF16), proj, proj, proj)


INV_BASE = 8
DN_STEP_CHUNKS = 2


def _split_bf16(t):
    hi = t.astype(BF16)
    lo = (t - hi.astype(F32)).astype(BF16)
    return hi, lo


def _split3_bf16(t):
    hi = t.astype(BF16)
    r1 = t - hi.astype(F32)
    mid = r1.astype(BF16)
    lo = (r1 - mid.astype(F32)).astype(BF16)
    return hi, mid, lo


def _pair_blockdiag(t, halves):
    return jnp.concatenate([t * halves[0], t * halves[1]], axis=0)


def _pair_dot(a_terms, b_terms, halves):
    m = a_terms[0].shape[0]
    out = None
    for j, b in enumerate(b_terms):
        lhs = a_terms if j == 0 else a_terms[:1]
        prod = jnp.dot(jnp.concatenate(lhs, axis=0), _pair_blockdiag(b, halves),
                       preferred_element_type=F32)
        for i in range(len(lhs)):
            part = prod[i * m:(i + 1) * m]
            out = part if out is None else out + part
    return out


def _inverse_masks(r, s, c):
    same = lambda bs: (r // bs) == (s // bs)
    masks = [same(INV_BASE)]
    bs = INV_BASE
    while bs < c:
        masks.append(same(2 * bs) & jnp.logical_not(same(bs)))
        bs *= 2
    return [jnp.where(m, 1.0, 0.0).astype(BF16) for m in masks]


def _unit_lower_inverse(a_list, eye, masks, halves):
    c = a_list[0].shape[0]
    dot = lambda x, y: _pair_dot(x, y, halves)
    split = _split_bf16
    d = [a * masks[0] for a in a_list]
    p = [eye - di.astype(F32) for di in d]
    x = [dot((di,), (di,)) for di in d]
    both = [dot(split(jnp.concatenate([xi, pi], axis=0)), split(xi))
            for xi, pi in zip(x, p)]
    p = [pi + bi[c:] for pi, bi in zip(p, both)]
    p = [pi + dot(split(pi), split(bi[:c])) for pi, bi in zip(p, both)]
    for m in masks[1:]:
        ps = [split(pi) for pi in p]
        pm = [dot(psi, (a * m,)) for psi, a in zip(ps, a_list)]
        p = [pi - dot(split(pmi), psi) for pi, pmi, psi in zip(p, pm, ps)]
    return p


def _softplus(t):
    return jnp.maximum(t, 0.0) + jnp.log(1.0 + jnp.exp(-jnp.abs(t)))


def _cumsum_rows(lower, t):
    n = t.shape[1]
    parts = jnp.dot(lower.astype(BF16), jnp.concatenate(_split3_bf16(t), axis=1),
                    preferred_element_type=F32)
    return parts[:, :n] + parts[:, n:2 * n] + parts[:, 2 * n:]


def _cumsum_lanes(t, upper):
    m = t.shape[0]
    parts = jnp.dot(jnp.concatenate(_split3_bf16(t), axis=0), upper.astype(BF16),
                    preferred_element_type=F32)
    return parts[:m] + parts[m:2 * m] + parts[2 * m:]


def _dn_body(qkv_ref, gate_ref, ab_ref, shift_ref, cw_ref, alog_ref, dtb_ref,
             alogt_ref, dtbt_ref, nw_ref, o_ref, zbuf, state):
    ci = pl.program_id(1)
    c = DN_CHUNK
    nch = DN_STEP_CHUNKS
    rows = nch * c
    hk = DN_HEAD_K
    dv = DN_HEAD_V
    width = DN_WIDTH
    pad = BF16_ROWS
    hist = CONV_K - 1
    heads = range(DN_HEADS)
    pairs = range(DN_HEADS // 2)
    chunks = range(nch)

    @pl.when(ci == 0)
    def _():
        state[...] = jnp.zeros_like(state)
        zbuf[0:pad, :] = jnp.zeros((pad, 3 * width), BF16)

    zbuf[pad:pad + rows, :] = qkv_ref[...]
    shifted = jnp.dot(shift_ref[...], zbuf[...], preferred_element_type=F32)
    y = qkv_ref[...].astype(F32) * cw_ref[hist:hist + 1, :]
    for j in range(hist):
        y = y + shifted[j * rows:(j + 1) * rows] * cw_ref[j:j + 1, :]
    zbuf[0:pad, :] = zbuf[rows:rows + pad, :]
    y = _silu(y)

    n_qk = 2 * DN_HEADS
    sq = y[:, :2 * width] * y[:, :2 * width]
    sq = jnp.concatenate([sq[:, i * hk:(i + 1) * hk] for i in range(n_qk)], axis=0)
    sums = jnp.dot(jnp.concatenate(_split_bf16(sq), axis=0), jnp.ones((hk, hk), BF16),
                   preferred_element_type=F32)
    rnorm = lax.rsqrt(sums[:n_qk * rows] + sums[n_qk * rows:] + L2_EPS)

    r = lax.broadcasted_iota(jnp.int32, (c, 2 * c), 0)
    lane = lax.broadcasted_iota(jnp.int32, (c, 2 * c), 1)
    left = lane < c
    halves = (jnp.where(left, 1.0, 0.0).astype(BF16), jnp.where(left, 0.0, 1.0).astype(BF16))
    s = jnp.where(left, lane, lane - c)
    tri = r >= s
    strict = r > s
    eye = jnp.where(r == s, 1.0, 0.0).astype(F32)
    inv_masks = _inverse_masks(r, s, c)

    ab = ab_ref[...]
    g_col_raw = -jnp.exp(alog_ref[...]) * _softplus(ab + dtb_ref[...])
    beta_col = jax.nn.sigmoid(ab)
    rr = lax.broadcasted_iota(jnp.int32, (rows, rows), 0)
    ss = lax.broadcasted_iota(jnp.int32, (rows, rows), 1)
    lower = jnp.where((rr >= ss) & ((rr // c) == (ss // c)), 1.0, 0.0).astype(F32)
    g_col = _cumsum_rows(lower, g_col_raw)
    a_rows = ab.T[0:DN_HEADS]
    g_row_raw = -jnp.exp(alogt_ref[...]) * _softplus(a_rows + dtbt_ref[...])
    upper = jnp.where((rr <= ss) & ((rr // c) == (ss // c)), 1.0, 0.0).astype(F32)
    g_row = _cumsum_lanes(g_row_raw, upper)
    g_row_sw = pltpu.roll(g_row, c, axis=1)

    q, k, kb, vb, kbg, qg = [], [], [], [], [], []
    for h in heads:
        qh = y[:, h * hk:(h + 1) * hk]
        kh = y[:, width + h * hk:width + (h + 1) * hk]
        vh = y[:, 2 * width + h * dv:2 * width + (h + 1) * dv]
        qh = qh * rnorm[h * rows:(h + 1) * rows] * (hk ** -0.5)
        kh = kh * rnorm[(DN_HEADS + h) * rows:(DN_HEADS + h + 1) * rows]
        beta = beta_col[:, DN_HEADS + h:DN_HEADS + h + 1]
        eg = jnp.exp(g_col[:, h:h + 1])
        q.append(qh)
        k.append(kh)
        kb.append(kh * beta)
        vb.append(vh * beta)
        kbg.append(kb[h] * eg)
        qg.append(qh * eg)

    a_list, ai_list = [], []
    zeros_k = jnp.zeros((c, hk), F32)
    for cc in chunks:
        cs = slice(cc * c, (cc + 1) * c)
        for j in pairs:
            h0, h1 = 2 * j, 2 * j + 1
            gcp = jnp.where(left, jnp.broadcast_to(g_col[cs, h0:h0 + 1], (c, 2 * c)),
                            jnp.broadcast_to(g_col[cs, h1:h1 + 1], (c, 2 * c)))
            grp = (jnp.where(left[0:1], g_row[h0:h0 + 1], g_row_sw[h1:h1 + 1]) if cc == 0 else
                   jnp.where(left[0:1], g_row_sw[h0:h0 + 1], g_row[h1:h1 + 1]))
            decay = jnp.exp(jnp.where(tri, gcp - grp, -jnp.inf))
            lhs = jnp.concatenate([jnp.concatenate([kb[h0][cs], kb[h1][cs]], axis=1),
                                   jnp.concatenate([q[h0][cs], q[h1][cs]], axis=1)], axis=0)
            kbd = jnp.concatenate([jnp.concatenate([k[h0][cs], zeros_k], axis=1),
                                   jnp.concatenate([zeros_k, k[h1][cs]], axis=1)], axis=0)
            kq = lax.dot_general(lhs.astype(BF16), kbd.astype(BF16), (((1,), (1,)), ((), ())),
                                 preferred_element_type=F32)
            a_list.append(jnp.where(strict, kq[:c] * decay, 0.0).astype(BF16))
            ai_list.append(jnp.where(tri, kq[c:] * decay, 0.0).astype(BF16))

    t_list = _unit_lower_inverse(a_list, eye, inv_masks, halves)

    u = [[None] * DN_HEADS for _ in chunks]
    w = [[None] * DN_HEADS for _ in chunks]
    zeros_v = jnp.zeros((c, dv + hk), F32)
    for cc in chunks:
        cs = slice(cc * c, (cc + 1) * c)
        for j in pairs:
            h0, h1 = 2 * j, 2 * j + 1
            rhs = jnp.concatenate([jnp.concatenate([vb[h0][cs], kbg[h0][cs], zeros_v], axis=1),
                                   jnp.concatenate([zeros_v, vb[h1][cs], kbg[h1][cs]], axis=1)], axis=0)
            uw = jnp.dot(t_list[cc * len(pairs) + j].astype(BF16), rhs.astype(BF16),
                         preferred_element_type=F32)
            u[cc][h0], w[cc][h0] = uw[:, :dv], uw[:, dv:dv + hk]
            u[cc][h1], w[cc][h1] = uw[:, dv + hk:2 * dv + hk], uw[:, 2 * dv + hk:]

    st = [state[h] for h in heads]
    zeros_b = jnp.zeros((c, dv), BF16)
    for cc in chunks:
        cs = slice(cc * c, (cc + 1) * c)
        ws = [jnp.dot(jnp.concatenate([w[cc][h], qg[h][cs]], axis=0).astype(BF16), st[h].astype(BF16),
                      preferred_element_type=F32) for h in heads]
        v_new = [(u[cc][h] - ws[h][:c]).astype(BF16) for h in heads]
        intra = []
        for j in pairs:
            h0, h1 = 2 * j, 2 * j + 1
            vbd = jnp.concatenate([jnp.concatenate([v_new[h0], zeros_b], axis=1),
                                   jnp.concatenate([zeros_b, v_new[h1]], axis=1)], axis=0)
            intra.append(jnp.dot(ai_list[cc * len(pairs) + j], vbd, preferred_element_type=F32))
        for h in heads:
            j, half = divmod(h, 2)
            o = ws[h][c:] + intra[j][:, half * dv:(half + 1) * dv]
            gc = g_col[cs, h:h + 1]
            gl = gc[c - 1:c, :]
            kd = k[h][cs] * jnp.exp(gl - gc)
            st[h] = st[h] * jnp.exp(gl) + lax.dot_general(
                kd.astype(BF16), v_new[h], (((0,), (0,)), ((), ())), preferred_element_type=F32)
            var = jnp.mean(o * o, axis=-1, keepdims=True)
            on = o * lax.rsqrt(var + NORM_EPS) * nw_ref[...]
            hs = slice(h * dv, (h + 1) * dv)
            o_ref[cs, hs] = (on * _silu(gate_ref[cs, hs].astype(F32))).astype(o_ref.dtype)
    for h in heads:
        state[h] = st[h]


def _deltanet(proj, ab, conv_w, a_log, dt_bias, norm_w, batch, seq):
    t = batch * seq
    c = DN_CHUNK
    nch = DN_STEP_CHUNKS
    rows = nch * c
    ns = seq // rows
    width = DN_WIDTH
    row = lambda b, i: b * ns + i
    assert rows == LANES
    lane_pad = lambda p: jnp.pad(p.reshape(1, -1), ((0, 0), (0, LANES - p.shape[0])))
    const = lambda shape: pl.BlockSpec(shape, lambda b, i: (0,) * len(shape))
    hist = CONV_K - 1
    shift = np.zeros((hist * rows, BF16_ROWS + rows), np.float32)
    for j in range(hist):
        shift[j * rows + np.arange(rows), BF16_ROWS + np.arange(rows) - hist + j] = 1.0
    return pl.pallas_call(
        _dn_body,
        grid=(batch, ns),
        in_specs=[pl.BlockSpec((rows, 3 * width), lambda b, i: (row(b, i), COL_DN_QKV // (3 * width))),
                  pl.BlockSpec((rows, width), lambda b, i: (row(b, i), COL_DN_GATE // width)),
                  pl.BlockSpec((rows, LANES), lambda b, i: (row(b, i), 0)),
                  const((hist * rows, BF16_ROWS + rows)),
                  const((CONV_K, 3 * width)),
                  const((1, LANES)), const((1, LANES)),
                  const((DN_HEADS, 1)), const((DN_HEADS, 1)),
                  const((1, DN_HEAD_V))],
        out_specs=pl.BlockSpec((rows, width), lambda b, i: (row(b, i), 0)),
        out_shape=jax.ShapeDtypeStruct((t, width), BF16),
        scratch_shapes=[pltpu.VMEM((BF16_ROWS + rows, 3 * width), BF16),
                        pltpu.VMEM((DN_HEADS, DN_HEAD_K, DN_HEAD_V), F32)],
        compiler_params=_cparams(("arbitrary", "arbitrary")),
        name="dn",
    )(proj, proj, ab, jnp.asarray(shift, BF16), conv_w, lane_pad(a_log), lane_pad(dt_bias),
      a_log.reshape(-1, 1), dt_bias.reshape(-1, 1), norm_w.reshape(1, -1))


def _mix_out_body(h_ref, g_ref, a_ref, d_ref, wa_ref, wd_ref, o_ref):
    mixed = jnp.dot(a_ref[...], wa_ref[...], preferred_element_type=F32)
    mixed = mixed + jnp.dot(d_ref[...], wd_ref[...], preferred_element_type=F32)
    o_ref[...] = h_ref[...] + g_ref[0] * mixed


def _mix_out(h, mod3, seq, attn_out, dn_out, w_out):
    tm = TOKEN_TILE
    t, d = h.shape
    assert ATTN_WIDTH == DN_WIDTH
    return pl.pallas_call(
        _mix_out_body,
        grid=(t // tm,),
        in_specs=[pl.BlockSpec((tm, d), lambda i: (i, 0)),
                  pl.BlockSpec((1, 1, d), lambda i: (((i * tm) // seq) * N_MOD + 5, 0, 0)),
                  pl.BlockSpec((tm, ATTN_WIDTH), lambda i: (i, 0)),
                  pl.BlockSpec((tm, DN_WIDTH), lambda i: (i, 0)),
                  pl.BlockSpec((ATTN_WIDTH, d), lambda i: (0, 0)),
                  pl.BlockSpec((DN_WIDTH, d), lambda i: (1, 0))],
        out_specs=pl.BlockSpec((tm, d), lambda i: (i, 0)),
        out_shape=jax.ShapeDtypeStruct((t, d), F32),
        compiler_params=_cparams(("arbitrary",)),
        name="mix_out",
    )(h, mod3, attn_out, dn_out, w_out, w_out)


def kernel(x, c, positions, ada_w, ada_b, norm_ffn1, ffn1_w_gate, ffn1_w_up, ffn1_w_down, norm_mix, w_in, conv_w, a_log, dt_bias, attn_sinks, dn_norm_w, w_out, norm_ffn2, ffn2_w_gate, ffn2_w_up, ffn2_w_down, final_norm):
    batch, seq, d = x.shape
    depth = ada_w.shape[0]
    assert depth >= 1 and seq % ATTN_BLOCK == 0 and seq % TOKEN_TILE == 0
    assert seq % (DN_CHUNK * DN_STEP_CHUNKS) == 0
    t = batch * seq
    h = x.reshape(t, d)
    assert seq % FFN_TM == 0 and seq % MIX_IN_TM == 0 and COL_DN_AB % MIX_IN_TN == 0
    o_dq = ATTN_WIDTH + 2 * ATTN_KV_WIDTH
    o_ab = o_dq + 4 * DN_WIDTH
    for l in range(depth):
        mod3 = _adaln(c, ada_w[l], ada_b[l]).reshape(batch * N_MOD, 1, d)
        h = _ffn(h, mod3, seq, 0, norm_ffn1[l], ffn1_w_gate[l], ffn1_w_up[l], ffn1_w_down[l],
                 final_norm, False)
        w = w_in[l]
        w_main = jnp.concatenate([w[:, o_dq:o_ab].astype(BF16), w[:, :o_dq].astype(BF16)], axis=1)
        w_ab = jnp.pad(w[:, o_ab:].astype(BF16), ((0, 0), (0, LANES - 2 * DN_HEADS)))
        proj, ab = _mix_in(h, mod3, seq, norm_mix[l], w_main, w_ab)
        attn_out = _attention(proj, positions, attn_sinks[l], batch, seq)
        dn_out = _deltanet(proj, ab, conv_w[l], a_log[l], dt_bias[l], dn_norm_w[l], batch, seq)
        h = _mix_out(h, mod3, seq, attn_out, dn_out, w_out[l].astype(BF16))
        h = _ffn(h, mod3, seq, 6, norm_ffn2[l], ffn2_w_gate[l], ffn2_w_up[l], ffn2_w_down[l],
                 final_norm, l == depth - 1)
    return h.reshape(batch, seq, d)
```

```python
import functools

import numpy as np
import jax
import jax.numpy as jnp
from jax import lax
from jax.experimental import pallas as pl
from jax.experimental.pallas import tpu as pltpu

F32 = jnp.float32
BF16 = jnp.bfloat16

ATTN_HEADS = 16
ATTN_KV_HEADS = 2
ATTN_HEAD_DIM = 64
WINDOW = 128
ATTN_BLOCK = 128
ROPE_THETA = 500000.0
ROPE_DIM = ATTN_HEAD_DIM // 4
DN_HEADS = 8
DN_HEAD_K = 128
DN_HEAD_V = 128
DN_CHUNK = 64
CONV_K = 4
NORM_EPS = 1e-6
L2_EPS = 1e-6
N_MOD = 9

ATTN_WIDTH = ATTN_HEADS * ATTN_HEAD_DIM
ATTN_KV_WIDTH = ATTN_KV_HEADS * ATTN_HEAD_DIM
DN_WIDTH = DN_HEADS * DN_HEAD_V
IN_WIDTH = ATTN_WIDTH + 2 * ATTN_KV_WIDTH + 4 * DN_WIDTH + 2 * DN_HEADS
COL_DN_QKV = 0
COL_DN_GATE = 3 * DN_WIDTH
COL_ATTN_Q = 4 * DN_WIDTH
COL_ATTN_K = COL_ATTN_Q + ATTN_WIDTH
COL_ATTN_V = COL_ATTN_K + ATTN_KV_WIDTH
COL_DN_AB = COL_ATTN_V + ATTN_KV_WIDTH
LANES = 128
SUBLANES = 8
BF16_ROWS = 16
VMEM_LIMIT = 56 * 1024 * 1024
FFN_VMEM_LIMIT = 60 * 1024 * 1024
TOKEN_TILE = 512
FFN_TM = 1024
FFN_TF = 256
MIX_IN_TM = 1024
MIX_IN_TN = 1792
ADALN_TN = 1024
NORM_ROWS = 128


def _cparams(sem, vmem_limit=VMEM_LIMIT):
    return pltpu.CompilerParams(dimension_semantics=sem, vmem_limit_bytes=vmem_limit)


def _silu(t):
    return t * jax.nn.sigmoid(t)


def _norm_mod_to(u_ref, h_ref, nw, shift, scale):
    gain = nw * (1.0 + scale)
    for r in range(0, h_ref.shape[0], NORM_ROWS):
        h = h_ref[r:r + NORM_ROWS, :]
        var = jnp.mean(h * h, axis=-1, keepdims=True)
        u_ref[r:r + NORM_ROWS, :] = (h * lax.rsqrt(var + NORM_EPS) * gain + shift).astype(u_ref.dtype)


def _adaln_body(c_ref, w_ref, b_ref, o_ref):
    ca = _silu(c_ref[...]).astype(BF16)
    o_ref[...] = jnp.dot(ca, w_ref[...].astype(BF16), preferred_element_type=F32) + b_ref[...]


def _adaln(c, w, b):
    tn = ADALN_TN
    nb, d = c.shape
    n = w.shape[1]
    rows = -(-nb // SUBLANES) * SUBLANES
    c_pad = jnp.pad(c, ((0, rows - nb), (0, 0)))
    out = pl.pallas_call(
        _adaln_body,
        grid=(n // tn,),
        in_specs=[pl.BlockSpec((rows, d), lambda j: (0, 0)),
                  pl.BlockSpec((d, tn), lambda j: (0, j)),
                  pl.BlockSpec((1, tn), lambda j: (0, j))],
        out_specs=pl.BlockSpec((rows, tn), lambda j: (0, j)),
        out_shape=jax.ShapeDtypeStruct((rows, n), F32),
        compiler_params=_cparams(("arbitrary",)),
        name="adaln",
    )(c_pad, w, b.reshape(1, n))
    return out[:nb]


def _ffn_body(h_ref, nw_ref, sh_ref, sc_ref, g_ref, wg_ref, wu_ref, wd_ref, fn_ref,
              o_ref, u_scr, *, final_norm):
    j = pl.program_id(1)

    def down_proj():
        u = u_scr[...]
        gate = jnp.dot(u, wg_ref[...].astype(BF16), preferred_element_type=F32)
        up = jnp.dot(u, wu_ref[...].astype(BF16), preferred_element_type=F32)
        act = (_silu(gate) * up).astype(BF16)
        return jnp.dot(act, wd_ref[...], preferred_element_type=F32)

    @pl.when(j == 0)
    def _():
        _norm_mod_to(u_scr, h_ref, nw_ref[...], sh_ref[0], sc_ref[0])
        o_ref[...] = down_proj()

    @pl.when(j > 0)
    def _():
        o_ref[...] += down_proj()

    @pl.when(j == pl.num_programs(1) - 1)
    def _():
        hn = h_ref[...] + 0.5 * g_ref[0] * o_ref[...]
        if final_norm:
            var = jnp.mean(hn * hn, axis=-1, keepdims=True)
            hn = hn * lax.rsqrt(var + NORM_EPS) * fn_ref[...]
        o_ref[...] = hn


def _ffn(h, mod3, seq, k0, nw, wg, wu, wd, fn, final_norm):
    tm, tf = FFN_TM, FFN_TF
    t, d = h.shape
    dff = wg.shape[1]
    mod_spec = lambda k: pl.BlockSpec((1, 1, d), lambda i, j: (((i * tm) // seq) * N_MOD + k, 0, 0))
    return pl.pallas_call(
        functools.partial(_ffn_body, final_norm=final_norm),
        grid=(t // tm, dff // tf),
        in_specs=[pl.BlockSpec((tm, d), lambda i, j: (i, 0)),
                  pl.BlockSpec((1, d), lambda i, j: (0, 0)),
                  mod_spec(k0), mod_spec(k0 + 1), mod_spec(k0 + 2),
                  pl.BlockSpec((d, tf), lambda i, j: (0, j)),
                  pl.BlockSpec((d, tf), lambda i, j: (0, j)),
                  pl.BlockSpec((tf, d), lambda i, j: (j, 0)),
                  pl.BlockSpec((1, d), lambda i, j: (0, 0))],
        out_specs=pl.BlockSpec((tm, d), lambda i, j: (i, 0)),
        out_shape=jax.ShapeDtypeStruct((t, d), F32),
        scratch_shapes=[pltpu.VMEM((tm, d), BF16)],
        compiler_params=_cparams(("arbitrary", "arbitrary"), FFN_VMEM_LIMIT),
        name="ffn_final" if final_norm else "ffn",
    )(h, nw.reshape(1, d), mod3, mod3, mod3, wg, wu, wd, fn.reshape(1, d))


def _mix_in_body(h_ref, nw_ref, sh_ref, sc_ref, w_ref, wab_ref, o_ref, ab_ref, u_scr):
    j = pl.program_id(1)

    def project():
        o_ref[...] = jnp.dot(u_scr[...], w_ref[...], preferred_element_type=F32).astype(o_ref.dtype)

    @pl.when(j == 0)
    def _():
        _norm_mod_to(u_scr, h_ref, nw_ref[...], sh_ref[0], sc_ref[0])
        ab_ref[...] = jnp.dot(u_scr[...], wab_ref[...], preferred_element_type=F32)
        project()

    @pl.when(j > 0)
    def _():
        project()


def _mix_in(h, mod3, seq, nw, w_main, wab):
    tm, tn = MIX_IN_TM, MIX_IN_TN
    t, d = h.shape
    n = w_main.shape[1]
    mod_spec = lambda k: pl.BlockSpec((1, 1, d), lambda i, j: (((i * tm) // seq) * N_MOD + k, 0, 0))
    return pl.pallas_call(
        _mix_in_body,
        grid=(t // tm, n // tn),
        in_specs=[pl.BlockSpec((tm, d), lambda i, j: (i, 0)),
                  pl.BlockSpec((1, d), lambda i, j: (0, 0)),
                  mod_spec(3), mod_spec(4),
                  pl.BlockSpec((d, tn), lambda i, j: (0, j)),
                  pl.BlockSpec((d, LANES), lambda i, j: (0, 0))],
        out_specs=[pl.BlockSpec((tm, tn), lambda i, j: (i, j)),
                   pl.BlockSpec((tm, LANES), lambda i, j: (i, 0))],
        out_shape=[jax.ShapeDtypeStruct((t, n), BF16),
                   jax.ShapeDtypeStruct((t, LANES), F32)],
        scratch_shapes=[pltpu.VMEM((tm, d), BF16)],
        compiler_params=_cparams(("arbitrary", "arbitrary")),
        name="mix_in",
    )(h, nw.reshape(1, d), mod3, mod3, w_main, wab)


def _attn_body(sink_ref, pos_ref, invf_ref, rot_ref, q_ref, k_ref, v_ref, o_ref, kprev, vprev):
    n = pl.program_id(1)
    blk = ATTN_BLOCK
    hd = ATTN_HEAD_DIM
    pairs_per_kv = ATTN_HEADS // ATTN_KV_HEADS // 2
    n_pairs = ATTN_HEADS // 2

    @pl.when(n == 0)
    def _():
        kprev[...] = jnp.zeros_like(kprev)
        vprev[...] = jnp.zeros_like(vprev)

    ang = pos_ref[...].astype(F32) * invf_ref[...]
    cos = jnp.cos(ang)
    sin = jnp.sin(ang)
    rot = rot_ref[...]

    kin = k_ref[...]
    k = kin.astype(F32) * cos + jnp.dot(kin, rot, preferred_element_type=F32) * sin
    qin = jnp.concatenate([q_ref[:, p * LANES:(p + 1) * LANES] for p in range(n_pairs)], axis=0)
    cos_q = jnp.concatenate([cos] * n_pairs, axis=0)
    sin_q = jnp.concatenate([sin] * n_pairs, axis=0)
    q = (qin.astype(F32) * cos_q + jnp.dot(qin, rot, preferred_element_type=F32) * sin_q) * (hd ** -0.5)
    q = q.astype(BF16)

    lane2 = lax.broadcasted_iota(jnp.int32, (2 * blk, LANES), 1)
    left2 = jnp.where(lane2 < hd, 1.0, 0.0).astype(BF16)
    right2 = jnp.where(lane2 < hd, 0.0, 1.0).astype(BF16)
    v = v_ref[...]
    cur = [k.astype(BF16), pltpu.roll(k, hd, axis=1).astype(BF16)]
    cur_v = [v, pltpu.roll(v.astype(F32), hd, axis=1).astype(BF16)]
    kband = [jnp.concatenate([kprev[i], cur[i]], axis=0) for i in range(2)]
    vband = [jnp.concatenate([vprev[i], cur_v[i]], axis=0) for i in range(2)]
    ones_cols = jnp.concatenate([left2, right2], axis=0)

    qi = lax.broadcasted_iota(jnp.int32, (blk, blk), 0)
    kj = lax.broadcasted_iota(jnp.int32, (blk, blk), 1)
    upper = kj > qi
    left = kj < hd
    prev_bias = jnp.where(n > 0, 0.0, -jnp.inf)

    scores = []
    for g in range(ATTN_KV_HEADS):
        a, b = (0, 1) if g == 0 else (1, 0)
        kbd = jnp.concatenate([kband[a] * left2, kband[b] * right2], axis=0)
        qg = q[g * pairs_per_kv * blk:(g + 1) * pairs_per_kv * blk]
        scores.append(lax.dot_general(qg, kbd, (((1,), (1,)), ((), ())), preferred_element_type=F32))

    probs, mx = [], []
    for p in range(n_pairs):
        g, pp = divmod(p, pairs_per_kv)
        sp = scores[g][pp * blk:(pp + 1) * blk]
        tiles, ms = [], []
        for i in range(2):
            comb = jnp.where(upper, sp[:, 2 * i * blk:(2 * i + 1) * blk] + prev_bias,
                             sp[:, (2 * i + 1) * blk:(2 * i + 2) * blk])
            m = jnp.maximum(jnp.max(comb, axis=-1, keepdims=True), sink_ref[2 * p + i])
            e = jnp.exp(comb - m)
            tiles += [jnp.where(upper, e, 0.0).astype(BF16), jnp.where(upper, 0.0, e).astype(BF16)]
            ms.append(m)
        probs.append(jnp.concatenate(tiles, axis=1))
        mx.append(ms)

    for g in range(ATTN_KV_HEADS):
        a, b = (0, 1) if g == 0 else (1, 0)
        vbd = jnp.concatenate([vband[a] * left2, vband[b] * right2], axis=0)
        vext = jnp.concatenate([vbd, ones_cols], axis=1)
        pg = jnp.concatenate(probs[g * pairs_per_kv:(g + 1) * pairs_per_kv], axis=0)
        res = jnp.dot(pg, vext, preferred_element_type=F32)
        for pp in range(pairs_per_kv):
            p = g * pairs_per_kv + pp
            rp = res[pp * blk:(pp + 1) * blk]
            sink_mass = jnp.where(left, jnp.exp(sink_ref[2 * p] - mx[p][0]),
                                  jnp.exp(sink_ref[2 * p + 1] - mx[p][1]))
            o = rp[:, :LANES] / (rp[:, LANES:] + sink_mass)
            o_ref[:, p * LANES:(p + 1) * LANES] = o.astype(o_ref.dtype)

    for i in range(2):
        kprev[i] = cur[i]
        vprev[i] = cur_v[i]


def _attention(proj, positions, sinks, batch, seq):
    assert WINDOW == ATTN_BLOCK and 2 * ATTN_HEAD_DIM == LANES
    t = batch * seq
    nb = seq // ATTN_BLOCK
    blk = ATTN_BLOCK
    half = ROPE_DIM // 2
    inv_freq = ROPE_THETA ** (-jnp.arange(0, ROPE_DIM, 2, dtype=F32) / ROPE_DIM)
    head_pat = jnp.concatenate([inv_freq, inv_freq, jnp.zeros((ATTN_HEAD_DIM - 2 * half,), F32)])
    invf = jnp.tile(head_pat, LANES // ATTN_HEAD_DIM).reshape(1, LANES)
    rot = np.zeros((LANES, LANES), np.float32)
    for base in range(0, LANES, ATTN_HEAD_DIM):
        for d in range(half):
            rot[base + d + half, base + d] = -1.0
            rot[base + d, base + d + half] = 1.0
    kcol = COL_ATTN_K // LANES
    row = lambda b, n: b * nb + n
    return pl.pallas_call(
        _attn_body,
        grid=(batch, nb),
        in_specs=[pl.BlockSpec(memory_space=pltpu.SMEM),
                  pl.BlockSpec((blk, 1), lambda b, n: (row(b, n), 0)),
                  pl.BlockSpec((1, LANES), lambda b, n: (0, 0)),
                  pl.BlockSpec((LANES, LANES), lambda b, n: (0, 0)),
                  pl.BlockSpec((blk, ATTN_WIDTH), lambda b, n: (row(b, n), COL_ATTN_Q // ATTN_WIDTH)),
                  pl.BlockSpec((blk, LANES), lambda b, n: (row(b, n), kcol)),
                  pl.BlockSpec((blk, LANES), lambda b, n: (row(b, n), kcol + 1))],
        out_specs=pl.BlockSpec((blk, ATTN_WIDTH), lambda b, n: (row(b, n), 0)),
        out_shape=jax.ShapeDtypeStruct((t, ATTN_WIDTH), BF16),
        scratch_shapes=[pltpu.VMEM((2, blk, LANES), BF16), pltpu.VMEM((2, blk, LANES), BF16)],
        compiler_params=_cparams(("arbitrary", "arbitrary")),
        name="attn",
    )(sinks, positions.reshape(t, 1), invf, jnp.asarray(rot, BF16), proj, proj, proj)


INV_BASE = 8
DN_STEP_CHUNKS = 2


def _split_bf16(t):
    hi = t.astype(BF16)
    lo = (t - hi.astype(F32)).astype(BF16)
    return hi, lo


def _split3_bf16(t):
    hi = t.astype(BF16)
    r1 = t - hi.astype(F32)
    mid = r1.astype(BF16)
    lo = (r1 - mid.astype(F32)).astype(BF16)
    return hi, mid, lo


def _pair_blockdiag(t, halves):
    return jnp.concatenate([t * halves[0], t * halves[1]], axis=0)


def _pair_dot(a_terms, b_terms, halves):
    m = a_terms[0].shape[0]
    out = None
    for j, b in enumerate(b_terms):
        lhs = a_terms if j == 0 else a_terms[:1]
        prod = jnp.dot(jnp.concatenate(lhs, axis=0), _pair_blockdiag(b, halves),
                       preferred_element_type=F32)
        for i in range(len(lhs)):
            part = prod[i * m:(i + 1) * m]
            out = part if out is None else out + part
    return out


def _inverse_masks(r, s, c):
    same = lambda bs: (r // bs) == (s // bs)
    masks = [same(INV_BASE)]
    bs = INV_BASE
    while bs < c:
        masks.append(same(2 * bs) & jnp.logical_not(same(bs)))
        bs *= 2
    return [jnp.where(m, 1.0, 0.0).astype(BF16) for m in masks]


def _unit_lower_inverse(a_list, eye, masks, halves):
    c = a_list[0].shape[0]
    dot = lambda x, y: _pair_dot(x, y, halves)
    split = _split_bf16
    d = [a * masks[0] for a in a_list]
    p = [eye - di.astype(F32) for di in d]
    x = [dot((di,), (di,)) for di in d]
    both = [dot(split(jnp.concatenate([xi, pi], axis=0)), split(xi))
            for xi, pi in zip(x, p)]
    p = [pi + bi[c:] for pi, bi in zip(p, both)]
    p = [pi + dot(split(pi), split(bi[:c])) for pi, bi in zip(p, both)]
    for m in masks[1:]:
        ps = [split(pi) for pi in p]
        pm = [dot(psi, (a * m,)) for psi, a in zip(ps, a_list)]
        p = [pi - dot(split(pmi), psi) for pi, pmi, psi in zip(p, pm, ps)]
    return p


def _softplus(t):
    return jnp.maximum(t, 0.0) + jnp.log(1.0 + jnp.exp(-jnp.abs(t)))


def _cumsum_rows(lower, t):
    n = t.shape[1]
    parts = jnp.dot(lower.astype(BF16), jnp.concatenate(_split3_bf16(t), axis=1),
                    preferred_element_type=F32)
    return parts[:, :n] + parts[:, n:2 * n] + parts[:, 2 * n:]


def _cumsum_lanes(t, upper):
    m = t.shape[0]
    parts = jnp.dot(jnp.concatenate(_split3_bf16(t), axis=0), upper.astype(BF16),
                    preferred_element_type=F32)
    return parts[:m] + parts[m:2 * m] + parts[2 * m:]


def _dn_body(qkv_ref, gate_ref, ab_ref, shift_ref, cw_ref, alog_ref, dtb_ref,
             alogt_ref, dtbt_ref, nw_ref, o_ref, zbuf, state):
    ci = pl.program_id(1)
    c = DN_CHUNK
    nch = DN_STEP_CHUNKS
    rows = nch * c
    hk = DN_HEAD_K
    dv = DN_HEAD_V
    width = DN_WIDTH
    pad = BF16_ROWS
    hist = CONV_K - 1
    heads = range(DN_HEADS)
    pairs = range(DN_HEADS // 2)
    chunks = range(nch)

    @pl.when(ci == 0)
    def _():
        state[...] = jnp.zeros_like(state)
        zbuf[0:pad, :] = jnp.zeros((pad, 3 * width), BF16)

    zbuf[pad:pad + rows, :] = qkv_ref[...]
    shifted = jnp.dot(shift_ref[...], zbuf[...], preferred_element_type=F32)
    y = qkv_ref[...].astype(F32) * cw_ref[hist:hist + 1, :]
    for j in range(hist):
        y = y + shifted[j * rows:(j + 1) * rows] * cw_ref[j:j + 1, :]
    zbuf[0:pad, :] = zbuf[rows:rows + pad, :]
    y = _silu(y)

    n_qk = 2 * DN_HEADS
    sq = y[:, :2 * width] * y[:, :2 * width]
    sq = jnp.concatenate([sq[:, i * hk:(i + 1) * hk] for i in range(n_qk)], axis=0)
    sums = jnp.dot(jnp.concatenate(_split_bf16(sq), axis=0), jnp.ones((hk, hk), BF16),
                   preferred_element_type=F32)
    rnorm = lax.rsqrt(sums[:n_qk * rows] + sums[n_qk * rows:] + L2_EPS)

    r = lax.broadcasted_iota(jnp.int32, (c, 2 * c), 0)
    lane = lax.broadcasted_iota(jnp.int32, (c, 2 * c), 1)
    left = lane < c
    halves = (jnp.where(left, 1.0, 0.0).astype(BF16), jnp.where(left, 0.0, 1.0).astype(BF16))
    s = jnp.where(left, lane, lane - c)
    tri = r >= s
    strict = r > s
    eye = jnp.where(r == s, 1.0, 0.0).astype(F32)
    inv_masks = _inverse_masks(r, s, c)

    ab = ab_ref[...]
    g_col_raw = -jnp.exp(alog_ref[...]) * _softplus(ab + dtb_ref[...])
    beta_col = jax.nn.sigmoid(ab)
    rr = lax.broadcasted_iota(jnp.int32, (rows, rows), 0)
    ss = lax.broadcasted_iota(jnp.int32, (rows, rows), 1)
    lower = jnp.where((rr >= ss) & ((rr // c) == (ss // c)), 1.0, 0.0).astype(F32)
    g_col = _cumsum_rows(lower, g_col_raw)
    a_rows = ab.T[0:DN_HEADS]
    g_row_raw = -jnp.exp(alogt_ref[...]) * _softplus(a_rows + dtbt_ref[...])
    upper = jnp.where((rr <= ss) & ((rr // c) == (ss // c)), 1.0, 0.0).astype(F32)
    g_row = _cumsum_lanes(g_row_raw, upper)
    g_row_sw = pltpu.roll(g_row, c, axis=1)

    q, k, kb, vb, kbg, qg = [], [], [], [], [], []
    for h in heads:
        qh = y[:, h * hk:(h + 1) * hk]
        kh = y[:, width + h * hk:width + (h + 1) * hk]
        vh = y[:, 2 * width + h * dv:2 * width + (h + 1) * dv]
        qh = qh * rnorm[h * rows:(h + 1) * rows] * (hk ** -0.5)
        kh = kh * rnorm[(DN_HEADS + h) * rows:(DN_HEADS + h + 1) * rows]
        beta = beta_col[:, DN_HEADS + h:DN_HEADS + h + 1]
        eg = jnp.exp(g_col[:, h:h + 1])
        q.append(qh)
        k.append(kh)
        kb.append(kh * beta)
        vb.append(vh * beta)
        kbg.append(kb[h] * eg)
        qg.append(qh * eg)

    a_list, ai_list = [], []
    zeros_k = jnp.zeros((c, hk), F32)
    for cc in chunks:
        cs = slice(cc * c, (cc + 1) * c)
        for j in pairs:
            h0, h1 = 2 * j, 2 * j + 1
            gcp = jnp.where(left, jnp.broadcast_to(g_col[cs, h0:h0 + 1], (c, 2 * c)),
                            jnp.broadcast_to(g_col[cs, h1:h1 + 1], (c, 2 * c)))
            grp = (jnp.where(left[0:1], g_row[h0:h0 + 1], g_row_sw[h1:h1 + 1]) if cc == 0 else
                   jnp.where(left[0:1], g_row_sw[h0:h0 + 1], g_row[h1:h1 + 1]))
            decay = jnp.exp(jnp.where(tri, gcp - grp, -jnp.inf))
            lhs = jnp.concatenate([jnp.concatenate([kb[h0][cs], kb[h1][cs]], axis=1),
                                   jnp.concatenate([q[h0][cs], q[h1][cs]], axis=1)], axis=0)
            kbd = jnp.concatenate([jnp.concatenate([k[h0][cs], zeros_k], axis=1),
                                   jnp.concatenate([zeros_k, k[h1][cs]], axis=1)], axis=0)
            kq = lax.dot_general(lhs.astype(BF16), kbd.astype(BF16), (((1,), (1,)), ((), ())),
                                 preferred_element_type=F32)
            a_list.append(jnp.where(strict, kq[:c] * decay, 0.0).astype(BF16))
            ai_list.append(jnp.where(tri, kq[c:] * decay, 0.0).astype(BF16))

    t_list = _unit_lower_inverse(a_list, eye, inv_masks, halves)

    u = [[None] * DN_HEADS for _ in chunks]
    w = [[None] * DN_HEADS for _ in chunks]
    zeros_v = jnp.zeros((c, dv + hk), F32)
    for cc in chunks:
        cs = slice(cc * c, (cc + 1) * c)
        for j in pairs:
            h0, h1 = 2 * j, 2 * j + 1
            rhs = jnp.concatenate([jnp.concatenate([vb[h0][cs], kbg[h0][cs], zeros_v], axis=1),
                                   jnp.concatenate([zeros_v, vb[h1][cs], kbg[h1][cs]], axis=1)], axis=0)
            uw = jnp.dot(t_list[cc * len(pairs) + j].astype(BF16), rhs.astype(BF16),
                         preferred_element_type=F32)
            u[cc][h0], w[cc][h0] = uw[:, :dv], uw[:, dv:dv + hk]
            u[cc][h1], w[cc][h1] = uw[:, dv + hk:2 * dv + hk], uw[:, 2 * dv + hk:]

    st = [state[h] for h in heads]
    zeros_b = jnp.zeros((c, dv), BF16)
    for cc in chunks:
        cs = slice(cc * c, (cc + 1) * c)
        ws = [jnp.dot(jnp.concatenate([w[cc][h], qg[h][cs]], axis=0).astype(BF16), st[h].astype(BF16),
                      preferred_element_type=F32) for h in heads]
        v_new = [(u[cc][h] - ws[h][:c]).astype(BF16) for h in heads]
        intra = []
        for j in pairs:
            h0, h1 = 2 * j, 2 * j + 1
            vbd = jnp.concatenate([jnp.concatenate([v_new[h0], zeros_b], axis=1),
                                   jnp.concatenate([zeros_b, v_new[h1]], axis=1)], axis=0)
            intra.append(jnp.dot(ai_list[cc * len(pairs) + j], vbd, preferred_element_type=F32))
        for h in heads:
            j, half = divmod(h, 2)
            o = ws[h][c:] + intra[j][:, half * dv:(half + 1) * dv]
            gc = g_col[cs, h:h + 1]
            gl = gc[c - 1:c, :]
            kd = k[h][cs] * jnp.exp(gl - gc)
            st[h] = st[h] * jnp.exp(gl) + lax.dot_general(
                kd.astype(BF16), v_new[h], (((0,), (0,)), ((), ())), preferred_element_type=F32)
            var = jnp.mean(o * o, axis=-1, keepdims=True)
            on = o * lax.rsqrt(var + NORM_EPS) * nw_ref[...]
            hs = slice(h * dv, (h + 1) * dv)
            o_ref[cs, hs] = (on * _silu(gate_ref[cs, hs].astype(F32))).astype(o_ref.dtype)
    for h in heads:
        state[h] = st[h]


def _deltanet(proj, ab, conv_w, a_log, dt_bias, norm_w, batch, seq):
    t = batch * seq
    c = DN_CHUNK
    nch = DN_STEP_CHUNKS
    rows = nch * c
    ns = seq // rows
    width = DN_WIDTH
    row = lambda b, i: b * ns + i
    assert rows == LANES
    lane_pad = lambda p: jnp.pad(p.reshape(1, -1), ((0, 0), (0, LANES - p.shape[0])))
    const = lambda shape: pl.BlockSpec(shape, lambda b, i: (0,) * len(shape))
    hist = CONV_K - 1
    shift = np.zeros((hist * rows, BF16_ROWS + rows), np.float32)
    for j in range(hist):
        shift[j * rows + np.arange(rows), BF16_ROWS + np.arange(rows) - hist + j] = 1.0
    return pl.pallas_call(
        _dn_body,
        grid=(batch, ns),
        in_specs=[pl.BlockSpec((rows, 3 * width), lambda b, i: (row(b, i), COL_DN_QKV // (3 * width))),
                  pl.BlockSpec((rows, width), lambda b, i: (row(b, i), COL_DN_GATE // width)),
                  pl.BlockSpec((rows, LANES), lambda b, i: (row(b, i), 0)),
                  const((hist * rows, BF16_ROWS + rows)),
                  const((CONV_K, 3 * width)),
                  const((1, LANES)), const((1, LANES)),
                  const((DN_HEADS, 1)), const((DN_HEADS, 1)),
                  const((1, DN_HEAD_V))],
        out_specs=pl.BlockSpec((rows, width), lambda b, i: (row(b, i), 0)),
        out_shape=jax.ShapeDtypeStruct((t, width), BF16),
        scratch_shapes=[pltpu.VMEM((BF16_ROWS + rows, 3 * width), BF16),
                        pltpu.VMEM((DN_HEADS, DN_HEAD_K, DN_HEAD_V), F32)],
        compiler_params=_cparams(("arbitrary", "arbitrary")),
        name="dn",
    )(proj, proj, ab, jnp.asarray(shift, BF16), conv_w, lane_pad(a_log), lane_pad(dt_bias),
      a_log.reshape(-1, 1), dt_bias.reshape(-1, 1), norm_w.reshape(1, -1))


def _mix_out_body(h_ref, g_ref, a_ref, d_ref, wa_ref, wd_ref, o_ref):
    mixed = jnp.dot(a_ref[...], wa_ref[...], preferred_element_type=F32)
    mixed = mixed + jnp.dot(d_ref[...], wd_ref[...], preferred_element_type=F32)
    o_ref[...] = h_ref[...] + g_ref[0] * mixed


def _mix_out(h, mod3, seq, attn_out, dn_out, w_out):
    tm = TOKEN_TILE
    t, d = h.shape
    assert ATTN_WIDTH == DN_WIDTH
    return pl.pallas_call(
        _mix_out_body,
        grid=(t // tm,),
        in_specs=[pl.BlockSpec((tm, d), lambda i: (i, 0)),
                  pl.BlockSpec((1, 1, d), lambda i: (((i * tm) // seq) * N_MOD + 5, 0, 0)),
                  pl.BlockSpec((tm, ATTN_WIDTH), lambda i: (i, 0)),
                  pl.BlockSpec((tm, DN_WIDTH), lambda i: (i, 0)),
                  pl.BlockSpec((ATTN_WIDTH, d), lambda i: (0, 0)),
                  pl.BlockSpec((DN_WIDTH, d), lambda i: (1, 0))],
        out_specs=pl.BlockSpec((tm, d), lambda i: (i, 0)),
        out_shape=jax.ShapeDtypeStruct((t, d), F32),
        compiler_params=_cparams(("arbitrary",)),
        name="mix_out",
    )(h, mod3, attn_out, dn_out, w_out, w_out)


def kernel(x, c, positions, ada_w, ada_b, norm_ffn1, ffn1_w_gate, ffn1_w_up, ffn1_w_down, norm_mix, w_in, conv_w, a_log, dt_bias, attn_sinks, dn_norm_w, w_out, norm_ffn2, ffn2_w_gate, ffn2_w_up, ffn2_w_down, final_norm):
    batch, seq, d = x.shape
    depth = ada_w.shape[0]
    assert depth >= 1 and seq % ATTN_BLOCK == 0 and seq % TOKEN_TILE == 0
    assert seq % (DN_CHUNK * DN_STEP_CHUNKS) == 0
    t = batch * seq
    h = x.reshape(t, d)
    assert seq % FFN_TM == 0 and seq % MIX_IN_TM == 0 and COL_DN_AB % MIX_IN_TN == 0
    o_dq = ATTN_WIDTH + 2 * ATTN_KV_WIDTH
    o_ab = o_dq + 4 * DN_WIDTH
    for l in range(depth):
        mod3 = _adaln(c, ada_w[l], ada_b[l]).reshape(batch * N_MOD, 1, d)
        h = _ffn(h, mod3, seq, 0, norm_ffn1[l], ffn1_w_gate[l], ffn1_w_up[l],
                 ffn1_w_down[l].astype(BF16), final_norm, False)
        w = w_in[l]
        w_main = jnp.concatenate([w[:, o_dq:o_ab].astype(BF16), w[:, :o_dq].astype(BF16)], axis=1)
        w_ab = jnp.pad(w[:, o_ab:].astype(BF16), ((0, 0), (0, LANES - 2 * DN_HEADS)))
        proj, ab = _mix_in(h, mod3, seq, norm_mix[l], w_main, w_ab)
        attn_out = _attention(proj, positions, attn_sinks[l], batch, seq)
        dn_out = _deltanet(proj, ab, conv_w[l], a_log[l], dt_bias[l], dn_norm_w[l], batch, seq)
        h = _mix_out(h, mod3, seq, attn_out, dn_out, w_out[l].astype(BF16))
        h = _ffn(h, mod3, seq, 6, norm_ffn2[l], ffn2_w_gate[l], ffn2_w_up[l],
                 ffn2_w_down[l].astype(BF16), final_norm, l == depth - 1)
    return h.reshape(batch, seq, d)
```

```python
import functools

import numpy as np
import jax
import jax.numpy as jnp
from jax import lax
from jax.experimental import pallas as pl
from jax.experimental.pallas import tpu as pltpu

F32 = jnp.float32
BF16 = jnp.bfloat16

ATTN_HEADS = 16
ATTN_KV_HEADS = 2
ATTN_HEAD_DIM = 64
WINDOW = 128
ATTN_BLOCK = 128
ROPE_THETA = 500000.0
ROPE_DIM = ATTN_HEAD_DIM // 4
DN_HEADS = 8
DN_HEAD_K = 128
DN_HEAD_V = 128
DN_CHUNK = 64
CONV_K = 4
NORM_EPS = 1e-6
L2_EPS = 1e-6
N_MOD = 9

ATTN_WIDTH = ATTN_HEADS * ATTN_HEAD_DIM
ATTN_KV_WIDTH = ATTN_KV_HEADS * ATTN_HEAD_DIM
DN_WIDTH = DN_HEADS * DN_HEAD_V
IN_WIDTH = ATTN_WIDTH + 2 * ATTN_KV_WIDTH + 4 * DN_WIDTH + 2 * DN_HEADS
COL_DN_QKV = 0
COL_DN_GATE = 3 * DN_WIDTH
COL_ATTN_Q = 4 * DN_WIDTH
COL_ATTN_K = COL_ATTN_Q + ATTN_WIDTH
COL_ATTN_V = COL_ATTN_K + ATTN_KV_WIDTH
COL_DN_AB = COL_ATTN_V + ATTN_KV_WIDTH
LANES = 128
SUBLANES = 8
BF16_ROWS = 16
VMEM_LIMIT = 56 * 1024 * 1024
FFN_VMEM_LIMIT = 60 * 1024 * 1024
TOKEN_TILE = 512
FFN_TM = 1024
FFN_TF = 256
MIX_IN_TM = 1024
MIX_IN_TN = 1792
ADALN_TN = 1024
NORM_ROWS = 128


def _cparams(sem, vmem_limit=VMEM_LIMIT):
    return pltpu.CompilerParams(dimension_semantics=sem, vmem_limit_bytes=vmem_limit)


def _silu(t):
    return t * jax.nn.sigmoid(t)


def _norm_mod_to(u_ref, h_ref, nw, shift, scale):
    gain = nw * (1.0 + scale)
    for r in range(0, h_ref.shape[0], NORM_ROWS):
        h = h_ref[r:r + NORM_ROWS, :]
        var = jnp.mean(h * h, axis=-1, keepdims=True)
        u_ref[r:r + NORM_ROWS, :] = (h * lax.rsqrt(var + NORM_EPS) * gain + shift).astype(u_ref.dtype)


def _adaln_body(c_ref, w_ref, b_ref, o_ref):
    ca = _silu(c_ref[...]).astype(BF16)
    o_ref[...] = jnp.dot(ca, w_ref[...].astype(BF16), preferred_element_type=F32) + b_ref[...]


def _adaln(c, w, b):
    tn = ADALN_TN
    nb, d = c.shape
    n = w.shape[1]
    rows = -(-nb // SUBLANES) * SUBLANES
    c_pad = jnp.pad(c, ((0, rows - nb), (0, 0)))
    out = pl.pallas_call(
        _adaln_body,
        grid=(n // tn,),
        in_specs=[pl.BlockSpec((rows, d), lambda j: (0, 0)),
                  pl.BlockSpec((d, tn), lambda j: (0, j)),
                  pl.BlockSpec((1, tn), lambda j: (0, j))],
        out_specs=pl.BlockSpec((rows, tn), lambda j: (0, j)),
        out_shape=jax.ShapeDtypeStruct((rows, n), F32),
        compiler_params=_cparams(("arbitrary",)),
        name="adaln",
    )(c_pad, w, b.reshape(1, n))
    return out[:nb]


def _ffn_body(h_ref, nw_ref, sh_ref, sc_ref, g_ref, wg_ref, wu_ref, wd_ref, fn_ref,
              o_ref, u_scr, *, final_norm):
    j = pl.program_id(1)

    def down_proj():
        u = u_scr[...]
        gate = jnp.dot(u, wg_ref[...].astype(BF16), preferred_element_type=F32)
        up = jnp.dot(u, wu_ref[...].astype(BF16), preferred_element_type=F32)
        act = (_silu(gate) * up).astype(BF16)
        return jnp.dot(act, wd_ref[...].astype(BF16), preferred_element_type=F32)

    @pl.when(j == 0)
    def _():
        _norm_mod_to(u_scr, h_ref, nw_ref[...], sh_ref[0], sc_ref[0])
        o_ref[...] = down_proj()

    last = pl.num_programs(1) - 1

    @pl.when((j > 0) & (j < last))
    def _():
        o_ref[...] += down_proj()

    @pl.when(j == last)
    def _():
        hn = h_ref[...] + 0.5 * g_ref[0] * (o_ref[...] + down_proj())
        if final_norm:
            var = jnp.mean(hn * hn, axis=-1, keepdims=True)
            hn = hn * lax.rsqrt(var + NORM_EPS) * fn_ref[...]
        o_ref[...] = hn


def _ffn(h, mod3, seq, k0, nw, wg, wu, wd, fn, final_norm):
    tm, tf = FFN_TM, FFN_TF
    t, d = h.shape
    dff = wg.shape[1]
    mod_spec = lambda k: pl.BlockSpec((1, 1, d), lambda i, j: (((i * tm) // seq) * N_MOD + k, 0, 0))
    return pl.pallas_call(
        functools.partial(_ffn_body, final_norm=final_norm),
        grid=(t // tm, dff // tf),
        in_specs=[pl.BlockSpec((tm, d), lambda i, j: (i, 0)),
                  pl.BlockSpec((1, d), lambda i, j: (0, 0)),
                  mod_spec(k0), mod_spec(k0 + 1), mod_spec(k0 + 2),
                  pl.BlockSpec((d, tf), lambda i, j: (0, j)),
                  pl.BlockSpec((d, tf), lambda i, j: (0, j)),
                  pl.BlockSpec((tf, d), lambda i, j: (j, 0)),
                  pl.BlockSpec((1, d), lambda i, j: (0, 0))],
        out_specs=pl.BlockSpec((tm, d), lambda i, j: (i, 0)),
        out_shape=jax.ShapeDtypeStruct((t, d), F32),
        scratch_shapes=[pltpu.VMEM((tm, d), BF16)],
        compiler_params=_cparams(("arbitrary", "arbitrary"), FFN_VMEM_LIMIT),
        name="ffn_final" if final_norm else "ffn",
    )(h, nw.reshape(1, d), mod3, mod3, mod3, wg, wu, wd, fn.reshape(1, d))


def _mix_in_body(h_ref, nw_ref, sh_ref, sc_ref, w_ref, wab_ref, o_ref, ab_ref, u_scr):
    j = pl.program_id(1)

    def project():
        o_ref[...] = jnp.dot(u_scr[...], w_ref[...], preferred_element_type=F32).astype(o_ref.dtype)

    @pl.when(j == 0)
    def _():
        _norm_mod_to(u_scr, h_ref, nw_ref[...], sh_ref[0], sc_ref[0])
        ab_ref[...] = jnp.dot(u_scr[...], wab_ref[...], preferred_element_type=F32)
        project()

    @pl.when(j > 0)
    def _():
        project()


def _mix_in(h, mod3, seq, nw, w_main, wab):
    tm, tn = MIX_IN_TM, MIX_IN_TN
    t, d = h.shape
    n = w_main.shape[1]
    mod_spec = lambda k: pl.BlockSpec((1, 1, d), lambda i, j: (((i * tm) // seq) * N_MOD + k, 0, 0))
    return pl.pallas_call(
        _mix_in_body,
        grid=(t // tm, n // tn),
        in_specs=[pl.BlockSpec((tm, d), lambda i, j: (i, 0)),
                  pl.BlockSpec((1, d), lambda i, j: (0, 0)),
                  mod_spec(3), mod_spec(4),
                  pl.BlockSpec((d, tn), lambda i, j: (0, j)),
                  pl.BlockSpec((d, LANES), lambda i, j: (0, 0))],
        out_specs=[pl.BlockSpec((tm, tn), lambda i, j: (i, j)),
                   pl.BlockSpec((tm, LANES), lambda i, j: (i, 0))],
        out_shape=[jax.ShapeDtypeStruct((t, n), BF16),
                   jax.ShapeDtypeStruct((t, LANES), F32)],
        scratch_shapes=[pltpu.VMEM((tm, d), BF16)],
        compiler_params=_cparams(("arbitrary", "arbitrary")),
        name="mix_in",
    )(h, nw.reshape(1, d), mod3, mod3, w_main, wab)


def _attn_body(sink_ref, pos_ref, invf_ref, rot_ref, q_ref, k_ref, v_ref, o_ref, kprev, vprev):
    n = pl.program_id(1)
    blk = ATTN_BLOCK
    hd = ATTN_HEAD_DIM
    pairs_per_kv = ATTN_HEADS // ATTN_KV_HEADS // 2
    n_pairs = ATTN_HEADS // 2

    @pl.when(n == 0)
    def _():
        kprev[...] = jnp.zeros_like(kprev)
        vprev[...] = jnp.zeros_like(vprev)

    ang = pos_ref[...].astype(F32) * invf_ref[...]
    cos = jnp.cos(ang)
    sin = jnp.sin(ang)
    rot = rot_ref[...]

    kin = k_ref[...]
    k = kin.astype(F32) * cos + jnp.dot(kin, rot, preferred_element_type=F32) * sin
    qin = jnp.concatenate([q_ref[:, p * LANES:(p + 1) * LANES] for p in range(n_pairs)], axis=0)
    cos_q = jnp.concatenate([cos] * n_pairs, axis=0)
    sin_q = jnp.concatenate([sin] * n_pairs, axis=0)
    q = (qin.astype(F32) * cos_q + jnp.dot(qin, rot, preferred_element_type=F32) * sin_q) * (hd ** -0.5)
    q = q.astype(BF16)

    lane2 = lax.broadcasted_iota(jnp.int32, (2 * blk, LANES), 1)
    left2 = jnp.where(lane2 < hd, 1.0, 0.0).astype(BF16)
    right2 = jnp.where(lane2 < hd, 0.0, 1.0).astype(BF16)
    v = v_ref[...]
    cur = [k.astype(BF16), pltpu.roll(k, hd, axis=1).astype(BF16)]
    cur_v = [v, pltpu.roll(v.astype(F32), hd, axis=1).astype(BF16)]
    kband = [jnp.concatenate([kprev[i], cur[i]], axis=0) for i in range(2)]
    vband = [jnp.concatenate([vprev[i], cur_v[i]], axis=0) for i in range(2)]
    ones_cols = jnp.concatenate([left2, right2], axis=0)

    qi = lax.broadcasted_iota(jnp.int32, (blk, blk), 0)
    kj = lax.broadcasted_iota(jnp.int32, (blk, blk), 1)
    upper = kj > qi
    left = kj < hd
    prev_bias = jnp.where(n > 0, 0.0, -jnp.inf)

    scores = []
    for g in range(ATTN_KV_HEADS):
        a, b = (0, 1) if g == 0 else (1, 0)
        kbd = jnp.concatenate([kband[a] * left2, kband[b] * right2], axis=0)
        qg = q[g * pairs_per_kv * blk:(g + 1) * pairs_per_kv * blk]
        scores.append(lax.dot_general(qg, kbd, (((1,), (1,)), ((), ())), preferred_element_type=F32))

    probs, mx = [], []
    for p in range(n_pairs):
        g, pp = divmod(p, pairs_per_kv)
        sp = scores[g][pp * blk:(pp + 1) * blk]
        tiles, ms = [], []
        for i in range(2):
            comb = jnp.where(upper, sp[:, 2 * i * blk:(2 * i + 1) * blk] + prev_bias,
                             sp[:, (2 * i + 1) * blk:(2 * i + 2) * blk])
            m = jnp.maximum(jnp.max(comb, axis=-1, keepdims=True), sink_ref[2 * p + i])
            e = jnp.exp(comb - m)
            tiles += [jnp.where(upper, e, 0.0).astype(BF16), jnp.where(upper, 0.0, e).astype(BF16)]
            ms.append(m)
        probs.append(jnp.concatenate(tiles, axis=1))
        mx.append(ms)

    for g in range(ATTN_KV_HEADS):
        a, b = (0, 1) if g == 0 else (1, 0)
        vbd = jnp.concatenate([vband[a] * left2, vband[b] * right2], axis=0)
        vext = jnp.concatenate([vbd, ones_cols], axis=1)
        pg = jnp.concatenate(probs[g * pairs_per_kv:(g + 1) * pairs_per_kv], axis=0)
        res = jnp.dot(pg, vext, preferred_element_type=F32)
        for pp in range(pairs_per_kv):
            p = g * pairs_per_kv + pp
            rp = res[pp * blk:(pp + 1) * blk]
            sink_mass = jnp.where(left, jnp.exp(sink_ref[2 * p] - mx[p][0]),
                                  jnp.exp(sink_ref[2 * p + 1] - mx[p][1]))
            o = rp[:, :LANES] / (rp[:, LANES:] + sink_mass)
            o_ref[:, p * LANES:(p + 1) * LANES] = o.astype(o_ref.dtype)

    for i in range(2):
        kprev[i] = cur[i]
        vprev[i] = cur_v[i]


def _attention(proj, positions, sinks, batch, seq):
    assert WINDOW == ATTN_BLOCK and 2 * ATTN_HEAD_DIM == LANES
    t = batch * seq
    nb = seq // ATTN_BLOCK
    blk = ATTN_BLOCK
    half = ROPE_DIM // 2
    inv_freq = ROPE_THETA ** (-jnp.arange(0, ROPE_DIM, 2, dtype=F32) / ROPE_DIM)
    head_pat = jnp.concatenate([inv_freq, inv_freq, jnp.zeros((ATTN_HEAD_DIM - 2 * half,), F32)])
    invf = jnp.tile(head_pat, LANES // ATTN_HEAD_DIM).reshape(1, LANES)
    rot = np.zeros((LANES, LANES), np.float32)
    for base in range(0, LANES, ATTN_HEAD_DIM):
        for d in range(half):
            rot[base + d + half, base + d] = -1.0
            rot[base + d, base + d + half] = 1.0
    kcol = COL_ATTN_K // LANES
    row = lambda b, n: b * nb + n
    return pl.pallas_call(
        _attn_body,
        grid=(batch, nb),
        in_specs=[pl.BlockSpec(memory_space=pltpu.SMEM),
                  pl.BlockSpec((blk, 1), lambda b, n: (row(b, n), 0)),
                  pl.BlockSpec((1, LANES), lambda b, n: (0, 0)),
                  pl.BlockSpec((LANES, LANES), lambda b, n: (0, 0)),
                  pl.BlockSpec((blk, ATTN_WIDTH), lambda b, n: (row(b, n), COL_ATTN_Q // ATTN_WIDTH)),
                  pl.BlockSpec((blk, LANES), lambda b, n: (row(b, n), kcol)),
                  pl.BlockSpec((blk, LANES), lambda b, n: (row(b, n), kcol + 1))],
        out_specs=pl.BlockSpec((blk, ATTN_WIDTH), lambda b, n: (row(b, n), 0)),
        out_shape=jax.ShapeDtypeStruct((t, ATTN_WIDTH), BF16),
        scratch_shapes=[pltpu.VMEM((2, blk, LANES), BF16), pltpu.VMEM((2, blk, LANES), BF16)],
        compiler_params=_cparams(("arbitrary", "arbitrary")),
        name="attn",
    )(sinks, positions.reshape(t, 1), invf, jnp.asarray(rot, BF16), proj, proj, proj)


INV_BASE = 8
DN_STEP_CHUNKS = 2


def _split_bf16(t):
    hi = t.astype(BF16)
    lo = (t - hi.astype(F32)).astype(BF16)
    return hi, lo


def _split3_bf16(t):
    hi = t.astype(BF16)
    r1 = t - hi.astype(F32)
    mid = r1.astype(BF16)
    lo = (r1 - mid.astype(F32)).astype(BF16)
    return hi, mid, lo


def _pair_blockdiag(t, halves):
    return jnp.concatenate([t * halves[0], t * halves[1]], axis=0)


def _pair_dot(a_terms, b_terms, halves):
    m = a_terms[0].shape[0]
    out = None
    for j, b in enumerate(b_terms):
        lhs = a_terms if j == 0 else a_terms[:1]
        prod = jnp.dot(jnp.concatenate(lhs, axis=0), _pair_blockdiag(b, halves),
                       preferred_element_type=F32)
        for i in range(len(lhs)):
            part = prod[i * m:(i + 1) * m]
            out = part if out is None else out + part
    return out


def _inverse_masks(r, s, c):
    same = lambda bs: (r // bs) == (s // bs)
    masks = [same(INV_BASE)]
    bs = INV_BASE
    while bs < c:
        masks.append(same(2 * bs) & jnp.logical_not(same(bs)))
        bs *= 2
    return [jnp.where(m, 1.0, 0.0).astype(BF16) for m in masks]


def _unit_lower_inverse(a_list, eye, masks, halves):
    c = a_list[0].shape[0]
    dot = lambda x, y: _pair_dot(x, y, halves)
    split = _split_bf16
    d = [a * masks[0] for a in a_list]
    p = [eye - di.astype(F32) for di in d]
    x = [dot((di,), (di,)) for di in d]
    both = [dot(split(jnp.concatenate([xi, pi], axis=0)), split(xi))
            for xi, pi in zip(x, p)]
    p = [pi + bi[c:] for pi, bi in zip(p, both)]
    p = [pi + dot(split(pi), split(bi[:c])) for pi, bi in zip(p, both)]
    for m in masks[1:]:
        ps = [split(pi) for pi in p]
        pm = [dot(psi, (a * m,)) for psi, a in zip(ps, a_list)]
        p = [pi - dot(split(pmi), psi) for pi, pmi, psi in zip(p, pm, ps)]
    return p


def _softplus(t):
    return jnp.maximum(t, 0.0) + jnp.log(1.0 + jnp.exp(-jnp.abs(t)))


def _cumsum_rows(lower, t):
    n = t.shape[1]
    parts = jnp.dot(lower.astype(BF16), jnp.concatenate(_split3_bf16(t), axis=1),
                    preferred_element_type=F32)
    return parts[:, :n] + parts[:, n:2 * n] + parts[:, 2 * n:]


def _cumsum_lanes(t, upper):
    m = t.shape[0]
    parts = jnp.dot(jnp.concatenate(_split3_bf16(t), axis=0), upper.astype(BF16),
                    preferred_element_type=F32)
    return parts[:m] + parts[m:2 * m] + parts[2 * m:]


def _dn_body(qkv_ref, gate_ref, ab_ref, shift_ref, cw_ref, alog_ref, dtb_ref,
             alogt_ref, dtbt_ref, nw_ref, o_ref, zbuf, state):
    ci = pl.program_id(1)
    c = DN_CHUNK
    nch = DN_STEP_CHUNKS
    rows = nch * c
    hk = DN_HEAD_K
    dv = DN_HEAD_V
    width = DN_WIDTH
    pad = BF16_ROWS
    hist = CONV_K - 1
    heads = range(DN_HEADS)
    pairs = range(DN_HEADS // 2)
    chunks = range(nch)

    @pl.when(ci == 0)
    def _():
        state[...] = jnp.zeros_like(state)
        zbuf[0:pad, :] = jnp.zeros((pad, 3 * width), BF16)

    zbuf[pad:pad + rows, :] = qkv_ref[...]
    shifted = jnp.dot(shift_ref[...], zbuf[...], preferred_element_type=F32)
    y = qkv_ref[...].astype(F32) * cw_ref[hist:hist + 1, :]
    for j in range(hist):
        y = y + shifted[j * rows:(j + 1) * rows] * cw_ref[j:j + 1, :]
    zbuf[0:pad, :] = zbuf[rows:rows + pad, :]
    y = _silu(y)

    n_qk = 2 * DN_HEADS
    sq = y[:, :2 * width] * y[:, :2 * width]
    sq = jnp.concatenate([sq[:, i * hk:(i + 1) * hk] for i in range(n_qk)], axis=0)
    sums = jnp.dot(jnp.concatenate(_split_bf16(sq), axis=0), jnp.ones((hk, hk), BF16),
                   preferred_element_type=F32)
    rnorm = lax.rsqrt(sums[:n_qk * rows] + sums[n_qk * rows:] + L2_EPS)

    r = lax.broadcasted_iota(jnp.int32, (c, 2 * c), 0)
    lane = lax.broadcasted_iota(jnp.int32, (c, 2 * c), 1)
    left = lane < c
    halves = (jnp.where(left, 1.0, 0.0).astype(BF16), jnp.where(left, 0.0, 1.0).astype(BF16))
    s = jnp.where(left, lane, lane - c)
    tri = r >= s
    strict = r > s
    eye = jnp.where(r == s, 1.0, 0.0).astype(F32)
    inv_masks = _inverse_masks(r, s, c)

    ab = ab_ref[...]
    g_col_raw = -jnp.exp(alog_ref[...]) * _softplus(ab + dtb_ref[...])
    beta_col = jax.nn.sigmoid(ab)
    rr = lax.broadcasted_iota(jnp.int32, (rows, rows), 0)
    ss = lax.broadcasted_iota(jnp.int32, (rows, rows), 1)
    lower = jnp.where((rr >= ss) & ((rr // c) == (ss // c)), 1.0, 0.0).astype(F32)
    g_col = _cumsum_rows(lower, g_col_raw)
    a_rows = ab.T[0:DN_HEADS]
    g_row_raw = -jnp.exp(alogt_ref[...]) * _softplus(a_rows + dtbt_ref[...])
    upper = jnp.where((rr <= ss) & ((rr // c) == (ss // c)), 1.0, 0.0).astype(F32)
    g_row = _cumsum_lanes(g_row_raw, upper)
    g_row_sw = pltpu.roll(g_row, c, axis=1)

    q, k, kb, vb, kbg, qg = [], [], [], [], [], []
    for h in heads:
        qh = y[:, h * hk:(h + 1) * hk]
        kh = y[:, width + h * hk:width + (h + 1) * hk]
        vh = y[:, 2 * width + h * dv:2 * width + (h + 1) * dv]
        qh = qh * rnorm[h * rows:(h + 1) * rows] * (hk ** -0.5)
        kh = kh * rnorm[(DN_HEADS + h) * rows:(DN_HEADS + h + 1) * rows]
        beta = beta_col[:, DN_HEADS + h:DN_HEADS + h + 1]
        eg = jnp.exp(g_col[:, h:h + 1])
        q.append(qh)
        k.append(kh)
        kb.append(kh * beta)
        vb.append(vh * beta)
        kbg.append(kb[h] * eg)
        qg.append(qh * eg)

    a_list, ai_list = [], []
    zeros_k = jnp.zeros((c, hk), F32)
    for cc in chunks:
        cs = slice(cc * c, (cc + 1) * c)
        for j in pairs:
            h0, h1 = 2 * j, 2 * j + 1
            gcp = jnp.where(left, jnp.broadcast_to(g_col[cs, h0:h0 + 1], (c, 2 * c)),
                            jnp.broadcast_to(g_col[cs, h1:h1 + 1], (c, 2 * c)))
            grp = (jnp.where(left[0:1], g_row[h0:h0 + 1], g_row_sw[h1:h1 + 1]) if cc == 0 else
                   jnp.where(left[0:1], g_row_sw[h0:h0 + 1], g_row[h1:h1 + 1]))
            decay = jnp.exp(jnp.where(tri, gcp - grp, -jnp.inf))
            lhs = jnp.concatenate([jnp.concatenate([kb[h0][cs], kb[h1][cs]], axis=1),
                                   jnp.concatenate([q[h0][cs], q[h1][cs]], axis=1)], axis=0)
            kbd = jnp.concatenate([jnp.concatenate([k[h0][cs], zeros_k], axis=1),
                                   jnp.concatenate([zeros_k, k[h1][cs]], axis=1)], axis=0)
            kq = lax.dot_general(lhs.astype(BF16), kbd.astype(BF16), (((1,), (1,)), ((), ())),
                                 preferred_element_type=F32)
            a_list.append(jnp.where(strict, kq[:c] * decay, 0.0).astype(BF16))
            ai_list.append(jnp.where(tri, kq[c:] * decay, 0.0).astype(BF16))

    t_list = _unit_lower_inverse(a_list, eye, inv_masks, halves)

    u = [[None] * DN_HEADS for _ in chunks]
    w = [[None] * DN_HEADS for _ in chunks]
    zeros_v = jnp.zeros((c, dv + hk), F32)
    for cc in chunks:
        cs = slice(cc * c, (cc + 1) * c)
        for j in pairs:
            h0, h1 = 2 * j, 2 * j + 1
            rhs = jnp.concatenate([jnp.concatenate([vb[h0][cs], kbg[h0][cs], zeros_v], axis=1),
                                   jnp.concatenate([zeros_v, vb[h1][cs], kbg[h1][cs]], axis=1)], axis=0)
            uw = jnp.dot(t_list[cc * len(pairs) + j].astype(BF16), rhs.astype(BF16),
                         preferred_element_type=F32)
            u[cc][h0], w[cc][h0] = uw[:, :dv], uw[:, dv:dv + hk]
            u[cc][h1], w[cc][h1] = uw[:, dv + hk:2 * dv + hk], uw[:, 2 * dv + hk:]

    st = [state[h] for h in heads]
    zeros_b = jnp.zeros((c, dv), BF16)
    for cc in chunks:
        cs = slice(cc * c, (cc + 1) * c)
        ws = [jnp.dot(jnp.concatenate([w[cc][h], qg[h][cs]], axis=0).astype(BF16), st[h].astype(BF16),
                      preferred_element_type=F32) for h in heads]
        v_new = [(u[cc][h] - ws[h][:c]).astype(BF16) for h in heads]
        intra = []
        for j in pairs:
            h0, h1 = 2 * j, 2 * j + 1
            vbd = jnp.concatenate([jnp.concatenate([v_new[h0], zeros_b], axis=1),
                                   jnp.concatenate([zeros_b, v_new[h1]], axis=1)], axis=0)
            intra.append(jnp.dot(ai_list[cc * len(pairs) + j], vbd, preferred_element_type=F32))
        for h in heads:
            j, half = divmod(h, 2)
            o = ws[h][c:] + intra[j][:, half * dv:(half + 1) * dv]
            gc = g_col[cs, h:h + 1]
            gl = gc[c - 1:c, :]
            kd = k[h][cs] * jnp.exp(gl - gc)
            st[h] = st[h] * jnp.exp(gl) + lax.dot_general(
                kd.astype(BF16), v_new[h], (((0,), (0,)), ((), ())), preferred_element_type=F32)
            var = jnp.mean(o * o, axis=-1, keepdims=True)
            on = o * lax.rsqrt(var + NORM_EPS) * nw_ref[...]
            hs = slice(h * dv, (h + 1) * dv)
            o_ref[cs, hs] = (on * _silu(gate_ref[cs, hs].astype(F32))).astype(o_ref.dtype)
    for h in heads:
        state[h] = st[h]


def _deltanet(proj, ab, conv_w, a_log, dt_bias, norm_w, batch, seq):
    t = batch * seq
    c = DN_CHUNK
    nch = DN_STEP_CHUNKS
    rows = nch * c
    ns = seq // rows
    width = DN_WIDTH
    row = lambda b, i: b * ns + i
    assert rows == LANES
    lane_pad = lambda p: jnp.pad(p.reshape(1, -1), ((0, 0), (0, LANES - p.shape[0])))
    const = lambda shape: pl.BlockSpec(shape, lambda b, i: (0,) * len(shape))
    hist = CONV_K - 1
    shift = np.zeros((hist * rows, BF16_ROWS + rows), np.float32)
    for j in range(hist):
        shift[j * rows + np.arange(rows), BF16_ROWS + np.arange(rows) - hist + j] = 1.0
    return pl.pallas_call(
        _dn_body,
        grid=(batch, ns),
        in_specs=[pl.BlockSpec((rows, 3 * width), lambda b, i: (row(b, i), COL_DN_QKV // (3 * width))),
                  pl.BlockSpec((rows, width), lambda b, i: (row(b, i), COL_DN_GATE // width)),
                  pl.BlockSpec((rows, LANES), lambda b, i: (row(b, i), 0)),
                  const((hist * rows, BF16_ROWS + rows)),
                  const((CONV_K, 3 * width)),
                  const((1, LANES)), const((1, LANES)),
                  const((DN_HEADS, 1)), const((DN_HEADS, 1)),
                  const((1, DN_HEAD_V))],
        out_specs=pl.BlockSpec((rows, width), lambda b, i: (row(b, i), 0)),
        out_shape=jax.ShapeDtypeStruct((t, width), BF16),
        scratch_shapes=[pltpu.VMEM((BF16_ROWS + rows, 3 * width), BF16),
                        pltpu.VMEM((DN_HEADS, DN_HEAD_K, DN_HEAD_V), F32)],
        compiler_params=_cparams(("arbitrary", "arbitrary")),
        name="dn",
    )(proj, proj, ab, jnp.asarray(shift, BF16), conv_w, lane_pad(a_log), lane_pad(dt_bias),
      a_log.reshape(-1, 1), dt_bias.reshape(-1, 1), norm_w.reshape(1, -1))


def _mix_out_body(h_ref, g_ref, a_ref, d_ref, wa_ref, wd_ref, o_ref):
    mixed = jnp.dot(a_ref[...], wa_ref[...], preferred_element_type=F32)
    mixed = mixed + jnp.dot(d_ref[...], wd_ref[...], preferred_element_type=F32)
    o_ref[...] = h_ref[...] + g_ref[0] * mixed


def _mix_out(h, mod3, seq, attn_out, dn_out, w_out):
    tm = TOKEN_TILE
    t, d = h.shape
    assert ATTN_WIDTH == DN_WIDTH
    return pl.pallas_call(
        _mix_out_body,
        grid=(t // tm,),
        in_specs=[pl.BlockSpec((tm, d), lambda i: (i, 0)),
                  pl.BlockSpec((1, 1, d), lambda i: (((i * tm) // seq) * N_MOD + 5, 0, 0)),
                  pl.BlockSpec((tm, ATTN_WIDTH), lambda i: (i, 0)),
                  pl.BlockSpec((tm, DN_WIDTH), lambda i: (i, 0)),
                  pl.BlockSpec((ATTN_WIDTH, d), lambda i: (0, 0)),
                  pl.BlockSpec((DN_WIDTH, d), lambda i: (1, 0))],
        out_specs=pl.BlockSpec((tm, d), lambda i: (i, 0)),
        out_shape=jax.ShapeDtypeStruct((t, d), F32),
        compiler_params=_cparams(("arbitrary",)),
        name="mix_out",
    )(h, mod3, attn_out, dn_out, w_out, w_out)


def kernel(x, c, positions, ada_w, ada_b, norm_ffn1, ffn1_w_gate, ffn1_w_up, ffn1_w_down, norm_mix, w_in, conv_w, a_log, dt_bias, attn_sinks, dn_norm_w, w_out, norm_ffn2, ffn2_w_gate, ffn2_w_up, ffn2_w_down, final_norm):
    batch, seq, d = x.shape
    depth = ada_w.shape[0]
    assert depth >= 1 and seq % ATTN_BLOCK == 0 and seq % TOKEN_TILE == 0
    assert seq % (DN_CHUNK * DN_STEP_CHUNKS) == 0
    t = batch * seq
    h = x.reshape(t, d)
    assert seq % FFN_TM == 0 and seq % MIX_IN_TM == 0 and COL_DN_AB % MIX_IN_TN == 0
    o_dq = ATTN_WIDTH + 2 * ATTN_KV_WIDTH
    o_ab = o_dq + 4 * DN_WIDTH
    for l in range(depth):
        mod3 = _adaln(c, ada_w[l], ada_b[l]).reshape(batch * N_MOD, 1, d)
        h = _ffn(h, mod3, seq, 0, norm_ffn1[l], ffn1_w_gate[l], ffn1_w_up[l], ffn1_w_down[l],
                 final_norm, False)
        w = w_in[l]
        w_main = jnp.concatenate([w[:, o_dq:o_ab].astype(BF16), w[:, :o_dq].astype(BF16)], axis=1)
        w_ab = jnp.pad(w[:, o_ab:].astype(BF16), ((0, 0), (0, LANES - 2 * DN_HEADS)))
        proj, ab = _mix_in(h, mod3, seq, norm_mix[l], w_main, w_ab)
        attn_out = _attention(proj, positions, attn_sinks[l], batch, seq)
        dn_out = _deltanet(proj, ab, conv_w[l], a_log[l], dt_bias[l], dn_norm_w[l], batch, seq)
        h = _mix_out(h, mod3, seq, attn_out, dn_out, w_out[l].astype(BF16))
        h = _ffn(h, mod3, seq, 6, norm_ffn2[l], ffn2_w_gate[l], ffn2_w_up[l], ffn2_w_down[l],
                 final_norm, l == depth - 1)
    return h.reshape(batch, seq, d)
```

```python
import functools

import numpy as np
import jax
import jax.numpy as jnp
from jax import lax
from jax.experimental import pallas as pl
from jax.experimental.pallas import tpu as pltpu

F32 = jnp.float32
BF16 = jnp.bfloat16

ATTN_HEADS = 16
ATTN_KV_HEADS = 2
ATTN_HEAD_DIM = 64
WINDOW = 128
ATTN_BLOCK = 128
ROPE_THETA = 500000.0
ROPE_DIM = ATTN_HEAD_DIM // 4
DN_HEADS = 8
DN_HEAD_K = 128
DN_HEAD_V = 128
DN_CHUNK = 64
CONV_K = 4
NORM_EPS = 1e-6
L2_EPS = 1e-6
N_MOD = 9

ATTN_WIDTH = ATTN_HEADS * ATTN_HEAD_DIM
ATTN_KV_WIDTH = ATTN_KV_HEADS * ATTN_HEAD_DIM
DN_WIDTH = DN_HEADS * DN_HEAD_V
IN_WIDTH = ATTN_WIDTH + 2 * ATTN_KV_WIDTH + 4 * DN_WIDTH + 2 * DN_HEADS
COL_DN_QKV = 0
COL_DN_GATE = 3 * DN_WIDTH
COL_ATTN_Q = 4 * DN_WIDTH
COL_ATTN_K = COL_ATTN_Q + ATTN_WIDTH
COL_ATTN_V = COL_ATTN_K + ATTN_KV_WIDTH
COL_DN_AB = COL_ATTN_V + ATTN_KV_WIDTH
LANES = 128
SUBLANES = 8
BF16_ROWS = 16
VMEM_LIMIT = 56 * 1024 * 1024
FFN_VMEM_LIMIT = 60 * 1024 * 1024
TOKEN_TILE = 512
FFN_TM = 1024
FFN_TF = 256
MIX_IN_TM = 1024
MIX_IN_TN = 1792
ADALN_TN = 1024
NORM_ROWS = 128


def _cparams(sem, vmem_limit=VMEM_LIMIT):
    return pltpu.CompilerParams(dimension_semantics=sem, vmem_limit_bytes=vmem_limit)


def _silu(t):
    return t * jax.nn.sigmoid(t)


def _norm_mod_to(u_ref, h_ref, nw, shift, scale):
    gain = nw * (1.0 + scale)
    for r in range(0, h_ref.shape[0], NORM_ROWS):
        h = h_ref[r:r + NORM_ROWS, :]
        var = jnp.mean(h * h, axis=-1, keepdims=True)
        u_ref[r:r + NORM_ROWS, :] = (h * lax.rsqrt(var + NORM_EPS) * gain + shift).astype(u_ref.dtype)


def _adaln_body(c_ref, w_ref, b_ref, o_ref):
    ca = _silu(c_ref[...]).astype(BF16)
    o_ref[...] = jnp.dot(ca, w_ref[...].astype(BF16), preferred_element_type=F32) + b_ref[...]


def _adaln(c, w, b):
    tn = ADALN_TN
    nb, d = c.shape
    n = w.shape[1]
    rows = -(-nb // SUBLANES) * SUBLANES
    c_pad = jnp.pad(c, ((0, rows - nb), (0, 0)))
    out = pl.pallas_call(
        _adaln_body,
        grid=(n // tn,),
        in_specs=[pl.BlockSpec((rows, d), lambda j: (0, 0)),
                  pl.BlockSpec((d, tn), lambda j: (0, j)),
                  pl.BlockSpec((1, tn), lambda j: (0, j))],
        out_specs=pl.BlockSpec((rows, tn), lambda j: (0, j)),
        out_shape=jax.ShapeDtypeStruct((rows, n), F32),
        compiler_params=_cparams(("arbitrary",)),
        name="adaln",
    )(c_pad, w, b.reshape(1, n))
    return out[:nb]


def _ffn_body(h_ref, nw_ref, sh_ref, sc_ref, g_ref, wg_ref, wu_ref, wd_ref, fn_ref,
              o_ref, u_scr, *, final_norm):
    j = pl.program_id(1)

    def down_proj():
        u = u_scr[...]
        gate = jnp.dot(u, wg_ref[...].astype(BF16), preferred_element_type=F32)
        up = jnp.dot(u, wu_ref[...].astype(BF16), preferred_element_type=F32)
        act = (_silu(gate) * up).astype(BF16)
        return jnp.dot(act, wd_ref[...].astype(BF16), preferred_element_type=F32)

    @pl.when(j == 0)
    def _():
        _norm_mod_to(u_scr, h_ref, nw_ref[...], sh_ref[0], sc_ref[0])
        o_ref[...] = down_proj()

    last = pl.num_programs(1) - 1

    @pl.when((j > 0) & (j < last))
    def _():
        o_ref[...] += down_proj()

    @pl.when(j == last)
    def _():
        hn = h_ref[...] + 0.5 * g_ref[0] * (o_ref[...] + down_proj())
        if final_norm:
            var = jnp.mean(hn * hn, axis=-1, keepdims=True)
            hn = hn * lax.rsqrt(var + NORM_EPS) * fn_ref[...]
        o_ref[...] = hn


def _ffn(h, mod3, seq, k0, nw, wg, wu, wd, fn, final_norm):
    tm, tf = FFN_TM, FFN_TF
    t, d = h.shape
    dff = wg.shape[1]
    mod_spec = lambda k: pl.BlockSpec((1, 1, d), lambda i, j: (((i * tm) // seq) * N_MOD + k, 0, 0))
    return pl.pallas_call(
        functools.partial(_ffn_body, final_norm=final_norm),
        grid=(t // tm, dff // tf),
        in_specs=[pl.BlockSpec((tm, d), lambda i, j: (i, 0)),
                  pl.BlockSpec((1, d), lambda i, j: (0, 0)),
                  mod_spec(k0), mod_spec(k0 + 1), mod_spec(k0 + 2),
                  pl.BlockSpec((d, tf), lambda i, j: (0, j)),
                  pl.BlockSpec((d, tf), lambda i, j: (0, j)),
                  pl.BlockSpec((tf, d), lambda i, j: (j, 0)),
                  pl.BlockSpec((1, d), lambda i, j: (0, 0))],
        out_specs=pl.BlockSpec((tm, d), lambda i, j: (i, 0)),
        out_shape=jax.ShapeDtypeStruct((t, d), F32),
        scratch_shapes=[pltpu.VMEM((tm, d), BF16)],
        compiler_params=_cparams(("arbitrary", "arbitrary"), FFN_VMEM_LIMIT),
        name="ffn_final" if final_norm else "ffn",
    )(h, nw.reshape(1, d), mod3, mod3, mod3, wg, wu, wd, fn.reshape(1, d))


def _mix_in_body(h_ref, nw_ref, sh_ref, sc_ref, w_ref, wab_ref, o_ref, ab_ref, u_scr):
    j = pl.program_id(1)

    def project():
        o_ref[...] = jnp.dot(u_scr[...], w_ref[...], preferred_element_type=F32).astype(o_ref.dtype)

    @pl.when(j == 0)
    def _():
        _norm_mod_to(u_scr, h_ref, nw_ref[...], sh_ref[0], sc_ref[0])
        ab_ref[...] = jnp.dot(u_scr[...], wab_ref[...], preferred_element_type=F32)
        project()

    @pl.when(j > 0)
    def _():
        project()


def _mix_in(h, mod3, seq, nw, w_main, wab):
    tm, tn = MIX_IN_TM, MIX_IN_TN
    t, d = h.shape
    n = w_main.shape[1]
    mod_spec = lambda k: pl.BlockSpec((1, 1, d), lambda i, j: (((i * tm) // seq) * N_MOD + k, 0, 0))
    return pl.pallas_call(
        _mix_in_body,
        grid=(t // tm, n // tn),
        in_specs=[pl.BlockSpec((tm, d), lambda i, j: (i, 0)),
                  pl.BlockSpec((1, d), lambda i, j: (0, 0)),
                  mod_spec(3), mod_spec(4),
                  pl.BlockSpec((d, tn), lambda i, j: (0, j)),
                  pl.BlockSpec((d, LANES), lambda i, j: (0, 0))],
        out_specs=[pl.BlockSpec((tm, tn), lambda i, j: (i, j)),
                   pl.BlockSpec((tm, LANES), lambda i, j: (i, 0))],
        out_shape=[jax.ShapeDtypeStruct((t, n), BF16),
                   jax.ShapeDtypeStruct((t, LANES), F32)],
        scratch_shapes=[pltpu.VMEM((tm, d), BF16)],
        compiler_params=_cparams(("arbitrary", "arbitrary")),
        name="mix_in",
    )(h, nw.reshape(1, d), mod3, mod3, w_main, wab)


def _attn_body(sink_ref, pos_ref, invf_ref, rot_ref, q_ref, k_ref, v_ref, o_ref, kprev, vprev):
    n = pl.program_id(1)
    blk = ATTN_BLOCK
    hd = ATTN_HEAD_DIM
    pairs_per_kv = ATTN_HEADS // ATTN_KV_HEADS // 2
    n_pairs = ATTN_HEADS // 2

    @pl.when(n == 0)
    def _():
        kprev[...] = jnp.zeros_like(kprev)
        vprev[...] = jnp.zeros_like(vprev)

    ang = pos_ref[...].astype(F32) * invf_ref[...]
    cos = jnp.cos(ang)
    sin = jnp.sin(ang)
    rot = rot_ref[...]

    kin = k_ref[...]
    k = kin.astype(F32) * cos + jnp.dot(kin, rot, preferred_element_type=F32) * sin
    qin = jnp.concatenate([q_ref[:, p * LANES:(p + 1) * LANES] for p in range(n_pairs)], axis=0)
    cos_q = jnp.concatenate([cos] * n_pairs, axis=0)
    sin_q = jnp.concatenate([sin] * n_pairs, axis=0)
    q = (qin.astype(F32) * cos_q + jnp.dot(qin, rot, preferred_element_type=F32) * sin_q) * (hd ** -0.5)
    q = q.astype(BF16)

    lane2 = lax.broadcasted_iota(jnp.int32, (2 * blk, LANES), 1)
    left2 = jnp.where(lane2 < hd, 1.0, 0.0).astype(BF16)
    right2 = jnp.where(lane2 < hd, 0.0, 1.0).astype(BF16)
    v = v_ref[...]
    cur = [k.astype(BF16), pltpu.roll(k, hd, axis=1).astype(BF16)]
    cur_v = [v, pltpu.roll(v.astype(F32), hd, axis=1).astype(BF16)]
    kband = [jnp.concatenate([kprev[i], cur[i]], axis=0) for i in range(2)]
    vband = [jnp.concatenate([vprev[i], cur_v[i]], axis=0) for i in range(2)]
    ones_cols = jnp.concatenate([left2, right2], axis=0)

    qi = lax.broadcasted_iota(jnp.int32, (blk, blk), 0)
    kj = lax.broadcasted_iota(jnp.int32, (blk, blk), 1)
    upper = kj > qi
    left = kj < hd
    prev_bias = jnp.where(n > 0, 0.0, -jnp.inf)

    scores = []
    for g in range(ATTN_KV_HEADS):
        a, b = (0, 1) if g == 0 else (1, 0)
        kbd = jnp.concatenate([kband[a] * left2, kband[b] * right2], axis=0)
        qg = q[g * pairs_per_kv * blk:(g + 1) * pairs_per_kv * blk]
        scores.append(lax.dot_general(qg, kbd, (((1,), (1,)), ((), ())), preferred_element_type=F32))

    probs, mx = [], []
    for p in range(n_pairs):
        g, pp = divmod(p, pairs_per_kv)
        sp = scores[g][pp * blk:(pp + 1) * blk]
        tiles, ms = [], []
        for i in range(2):
            comb = jnp.where(upper, sp[:, 2 * i * blk:(2 * i + 1) * blk] + prev_bias,
                             sp[:, (2 * i + 1) * blk:(2 * i + 2) * blk])
            m = jnp.maximum(jnp.max(comb, axis=-1, keepdims=True), sink_ref[2 * p + i])
            e = jnp.exp(comb - m)
            tiles += [jnp.where(upper, e, 0.0).astype(BF16), jnp.where(upper, 0.0, e).astype(BF16)]
            ms.append(m)
        probs.append(jnp.concatenate(tiles, axis=1))
        mx.append(ms)

    for g in range(ATTN_KV_HEADS):
        a, b = (0, 1) if g == 0 else (1, 0)
        vbd = jnp.concatenate([vband[a] * left2, vband[b] * right2], axis=0)
        vext = jnp.concatenate([vbd, ones_cols], axis=1)
        pg = jnp.concatenate(probs[g * pairs_per_kv:(g + 1) * pairs_per_kv], axis=0)
        res = jnp.dot(pg, vext, preferred_element_type=F32)
        for pp in range(pairs_per_kv):
            p = g * pairs_per_kv + pp
            rp = res[pp * blk:(pp + 1) * blk]
            sink_mass = jnp.where(left, jnp.exp(sink_ref[2 * p] - mx[p][0]),
                                  jnp.exp(sink_ref[2 * p + 1] - mx[p][1]))
            o = rp[:, :LANES] / (rp[:, LANES:] + sink_mass)
            o_ref[:, p * LANES:(p + 1) * LANES] = o.astype(o_ref.dtype)

    for i in range(2):
        kprev[i] = cur[i]
        vprev[i] = cur_v[i]


def _attention(proj, positions, sinks, batch, seq):
    assert WINDOW == ATTN_BLOCK and 2 * ATTN_HEAD_DIM == LANES
    t = batch * seq
    nb = seq // ATTN_BLOCK
    blk = ATTN_BLOCK
    half = ROPE_DIM // 2
    inv_freq = ROPE_THETA ** (-jnp.arange(0, ROPE_DIM, 2, dtype=F32) / ROPE_DIM)
    head_pat = jnp.concatenate([inv_freq, inv_freq, jnp.zeros((ATTN_HEAD_DIM - 2 * half,), F32)])
    invf = jnp.tile(head_pat, LANES // ATTN_HEAD_DIM).reshape(1, LANES)
    rot = np.zeros((LANES, LANES), np.float32)
    for base in range(0, LANES, ATTN_HEAD_DIM):
        for d in range(half):
            rot[base + d + half, base + d] = -1.0
            rot[base + d, base + d + half] = 1.0
    kcol = COL_ATTN_K // LANES
    row = lambda b, n: b * nb + n
    return pl.pallas_call(
        _attn_body,
        grid=(batch, nb),
        in_specs=[pl.BlockSpec(memory_space=pltpu.SMEM),
                  pl.BlockSpec((blk, 1), lambda b, n: (row(b, n), 0)),
                  pl.BlockSpec((1, LANES), lambda b, n: (0, 0)),
                  pl.BlockSpec((LANES, LANES), lambda b, n: (0, 0)),
                  pl.BlockSpec((blk, ATTN_WIDTH), lambda b, n: (row(b, n), COL_ATTN_Q // ATTN_WIDTH)),
                  pl.BlockSpec((blk, LANES), lambda b, n: (row(b, n), kcol)),
                  pl.BlockSpec((blk, LANES), lambda b, n: (row(b, n), kcol + 1))],
        out_specs=pl.BlockSpec((blk, ATTN_WIDTH), lambda b, n: (row(b, n), 0)),
        out_shape=jax.ShapeDtypeStruct((t, ATTN_WIDTH), BF16),
        scratch_shapes=[pltpu.VMEM((2, blk, LANES), BF16), pltpu.VMEM((2, blk, LANES), BF16)],
        compiler_params=_cparams(("arbitrary", "arbitrary")),
        name="attn",
    )(sinks, positions.reshape(t, 1), invf, jnp.asarray(rot, BF16), proj, proj, proj)


INV_BASE = 8
DN_STEP_CHUNKS = 2


def _split_bf16(t):
    hi = t.astype(BF16)
    lo = (t - hi.astype(F32)).astype(BF16)
    return hi, lo


def _split3_bf16(t):
    hi = t.astype(BF16)
    r1 = t - hi.astype(F32)
    mid = r1.astype(BF16)
    lo = (r1 - mid.astype(F32)).astype(BF16)
    return hi, mid, lo


def _pair_blockdiag(t, halves):
    return jnp.concatenate([t * halves[0], t * halves[1]], axis=0)


def _pair_dot(a_terms, b_terms, halves):
    m = a_terms[0].shape[0]
    out = None
    for j, b in enumerate(b_terms):
        lhs = a_terms if j == 0 else a_terms[:1]
        prod = jnp.dot(jnp.concatenate(lhs, axis=0), _pair_blockdiag(b, halves),
                       preferred_element_type=F32)
        for i in range(len(lhs)):
            part = prod[i * m:(i + 1) * m]
            out = part if out is None else out + part
    return out


def _inverse_masks(r, s, c):
    same = lambda bs: (r // bs) == (s // bs)
    masks = [same(INV_BASE)]
    bs = INV_BASE
    while bs < c:
        masks.append(same(2 * bs) & jnp.logical_not(same(bs)))
        bs *= 2
    return [jnp.where(m, 1.0, 0.0).astype(BF16) for m in masks]


def _unit_lower_inverse(a_list, eye, masks, halves):
    c = a_list[0].shape[0]
    dot = lambda x, y: _pair_dot(x, y, halves)
    split = _split_bf16
    d = [a * masks[0] for a in a_list]
    p = [eye - di.astype(F32) for di in d]
    x = [dot((di,), (di,)) for di in d]
    both = [dot(split(jnp.concatenate([xi, pi], axis=0)), split(xi))
            for xi, pi in zip(x, p)]
    p = [pi + bi[c:] for pi, bi in zip(p, both)]
    p = [pi + dot(split(pi), split(bi[:c])) for pi, bi in zip(p, both)]
    for m in masks[1:]:
        ps = [split(pi) for pi in p]
        pm = [dot(psi, (a * m,)) for psi, a in zip(ps, a_list)]
        p = [pi - dot(split(pmi), psi) for pi, pmi, psi in zip(p, pm, ps)]
    return p


def _softplus(t):
    return jnp.maximum(t, 0.0) + jnp.log(1.0 + jnp.exp(-jnp.abs(t)))


def _cumsum_rows(lower, t):
    n = t.shape[1]
    parts = jnp.dot(lower.astype(BF16), jnp.concatenate(_split3_bf16(t), axis=1),
                    preferred_element_type=F32)
    return parts[:, :n] + parts[:, n:2 * n] + parts[:, 2 * n:]


def _cumsum_lanes(t, upper):
    m = t.shape[0]
    parts = jnp.dot(jnp.concatenate(_split3_bf16(t), axis=0), upper.astype(BF16),
                    preferred_element_type=F32)
    return parts[:m] + parts[m:2 * m] + parts[2 * m:]


def _dn_body(qkv_ref, gate_ref, ab_ref, shift_ref, cw_ref, alog_ref, dtb_ref,
             alogt_ref, dtbt_ref, nw_ref, o_ref, zbuf, state):
    ci = pl.program_id(1)
    c = DN_CHUNK
    nch = DN_STEP_CHUNKS
    rows = nch * c
    hk = DN_HEAD_K
    dv = DN_HEAD_V
    width = DN_WIDTH
    pad = BF16_ROWS
    hist = CONV_K - 1
    heads = range(DN_HEADS)
    pairs = range(DN_HEADS // 2)
    chunks = range(nch)

    @pl.when(ci == 0)
    def _():
        state[...] = jnp.zeros_like(state)
        zbuf[0:pad, :] = jnp.zeros((pad, 3 * width), BF16)

    zbuf[pad:pad + rows, :] = qkv_ref[...]
    shifted = jnp.dot(shift_ref[...], zbuf[...], preferred_element_type=F32)
    y = qkv_ref[...].astype(F32) * cw_ref[hist:hist + 1, :]
    for j in range(hist):
        y = y + shifted[j * rows:(j + 1) * rows] * cw_ref[j:j + 1, :]
    zbuf[0:pad, :] = zbuf[rows:rows + pad, :]
    y = _silu(y)

    n_qk = 2 * DN_HEADS
    sq = y[:, :2 * width] * y[:, :2 * width]
    sq = jnp.concatenate([sq[:, i * hk:(i + 1) * hk] for i in range(n_qk)], axis=0)
    sums = jnp.dot(jnp.concatenate(_split_bf16(sq), axis=0), jnp.ones((hk, hk), BF16),
                   preferred_element_type=F32)
    rnorm = lax.rsqrt(sums[:n_qk * rows] + sums[n_qk * rows:] + L2_EPS)

    r = lax.broadcasted_iota(jnp.int32, (c, 2 * c), 0)
    lane = lax.broadcasted_iota(jnp.int32, (c, 2 * c), 1)
    left = lane < c
    halves = (jnp.where(left, 1.0, 0.0).astype(BF16), jnp.where(left, 0.0, 1.0).astype(BF16))
    s = jnp.where(left, lane, lane - c)
    tri = r >= s
    strict = r > s
    eye = jnp.where(r == s, 1.0, 0.0).astype(F32)
    inv_masks = _inverse_masks(r, s, c)

    ab = ab_ref[...]
    g_col_raw = -jnp.exp(alog_ref[...]) * _softplus(ab + dtb_ref[...])
    beta_col = jax.nn.sigmoid(ab)
    rr = lax.broadcasted_iota(jnp.int32, (rows, rows), 0)
    ss = lax.broadcasted_iota(jnp.int32, (rows, rows), 1)
    lower = jnp.where((rr >= ss) & ((rr // c) == (ss // c)), 1.0, 0.0).astype(F32)
    g_col = _cumsum_rows(lower, g_col_raw)
    a_rows = ab.T[0:DN_HEADS]
    g_row_raw = -jnp.exp(alogt_ref[...]) * _softplus(a_rows + dtbt_ref[...])
    upper = jnp.where((rr <= ss) & ((rr // c) == (ss // c)), 1.0, 0.0).astype(F32)
    g_row = _cumsum_lanes(g_row_raw, upper)
    g_row_sw = pltpu.roll(g_row, c, axis=1)

    q, k, kb, vb, kbg, qg = [], [], [], [], [], []
    for h in heads:
        qh = y[:, h * hk:(h + 1) * hk]
        kh = y[:, width + h * hk:width + (h + 1) * hk]
        vh = y[:, 2 * width + h * dv:2 * width + (h + 1) * dv]
        qh = qh * rnorm[h * rows:(h + 1) * rows] * (hk ** -0.5)
        kh = kh * rnorm[(DN_HEADS + h) * rows:(DN_HEADS + h + 1) * rows]
        beta = beta_col[:, DN_HEADS + h:DN_HEADS + h + 1]
        eg = jnp.exp(g_col[:, h:h + 1])
        q.append(qh)
        k.append(kh)
        kb.append(kh * beta)
        vb.append(vh * beta)
        kbg.append(kb[h] * eg)
        qg.append(qh * eg)

    a_list, ai_list = [], []
    zeros_k = jnp.zeros((c, hk), F32)
    for cc in chunks:
        cs = slice(cc * c, (cc + 1) * c)
        for j in pairs:
            h0, h1 = 2 * j, 2 * j + 1
            gcp = jnp.where(left, jnp.broadcast_to(g_col[cs, h0:h0 + 1], (c, 2 * c)),
                            jnp.broadcast_to(g_col[cs, h1:h1 + 1], (c, 2 * c)))
            grp = (jnp.where(left[0:1], g_row[h0:h0 + 1], g_row_sw[h1:h1 + 1]) if cc == 0 else
                   jnp.where(left[0:1], g_row_sw[h0:h0 + 1], g_row[h1:h1 + 1]))
            decay = jnp.exp(jnp.where(tri, gcp - grp, -jnp.inf))
            lhs = jnp.concatenate([jnp.concatenate([kb[h0][cs], kb[h1][cs]], axis=1),
                                   jnp.concatenate([q[h0][cs], q[h1][cs]], axis=1)], axis=0)
            kbd = jnp.concatenate([jnp.concatenate([k[h0][cs], zeros_k], axis=1),
                                   jnp.concatenate([zeros_k, k[h1][cs]], axis=1)], axis=0)
            kq = lax.dot_general(lhs.astype(BF16), kbd.astype(BF16), (((1,), (1,)), ((), ())),
                                 preferred_element_type=F32)
            a_list.append(jnp.where(strict, kq[:c] * decay, 0.0).astype(BF16))
            ai_list.append(jnp.where(tri, kq[c:] * decay, 0.0).astype(BF16))

    t_list = _unit_lower_inverse(a_list, eye, inv_masks, halves)

    u = [[None] * DN_HEADS for _ in chunks]
    w = [[None] * DN_HEADS for _ in chunks]
    zeros_v = jnp.zeros((c, dv + hk), F32)
    for cc in chunks:
        cs = slice(cc * c, (cc + 1) * c)
        for j in pairs:
            h0, h1 = 2 * j, 2 * j + 1
            rhs = jnp.concatenate([jnp.concatenate([vb[h0][cs], kbg[h0][cs], zeros_v], axis=1),
                                   jnp.concatenate([zeros_v, vb[h1][cs], kbg[h1][cs]], axis=1)], axis=0)
            uw = jnp.dot(t_list[cc * len(pairs) + j].astype(BF16), rhs.astype(BF16),
                         preferred_element_type=F32)
            u[cc][h0], w[cc][h0] = uw[:, :dv], uw[:, dv:dv + hk]
            u[cc][h1], w[cc][h1] = uw[:, dv + hk:2 * dv + hk], uw[:, 2 * dv + hk:]

    st = [state[h] for h in heads]
    zeros_b = jnp.zeros((c, dv), BF16)
    for cc in chunks:
        cs = slice(cc * c, (cc + 1) * c)
        ws = [jnp.dot(jnp.concatenate([w[cc][h], qg[h][cs]], axis=0).astype(BF16), st[h].astype(BF16),
                      preferred_element_type=F32) for h in heads]
        v_new = [(u[cc][h] - ws[h][:c]).astype(BF16) for h in heads]
        intra = []
        for j in pairs:
            h0, h1 = 2 * j, 2 * j + 1
            vbd = jnp.concatenate([jnp.concatenate([v_new[h0], zeros_b], axis=1),
                                   jnp.concatenate([zeros_b, v_new[h1]], axis=1)], axis=0)
            intra.append(jnp.dot(ai_list[cc * len(pairs) + j], vbd, preferred_element_type=F32))
        for h in heads:
            j, half = divmod(h, 2)
            o = ws[h][c:] + intra[j][:, half * dv:(half + 1) * dv]
            gc = g_col[cs, h:h + 1]
            gl = gc[c - 1:c, :]
            kd = k[h][cs] * jnp.exp(gl - gc)
            st[h] = st[h] * jnp.exp(gl) + lax.dot_general(
                kd.astype(BF16), v_new[h], (((0,), (0,)), ((), ())), preferred_element_type=F32)
            var = jnp.mean(o * o, axis=-1, keepdims=True)
            on = o * lax.rsqrt(var + NORM_EPS) * nw_ref[...]
            hs = slice(h * dv, (h + 1) * dv)
            o_ref[cs, hs] = (on * _silu(gate_ref[cs, hs].astype(F32))).astype(o_ref.dtype)
    for h in heads:
        state[h] = st[h]


def _deltanet(proj, ab, conv_w, a_log, dt_bias, norm_w, batch, seq):
    t = batch * seq
    c = DN_CHUNK
    nch = DN_STEP_CHUNKS
    rows = nch * c
    ns = seq // rows
    width = DN_WIDTH
    row = lambda b, i: b * ns + i
    assert rows == LANES
    lane_pad = lambda p: jnp.pad(p.reshape(1, -1), ((0, 0), (0, LANES - p.shape[0])))
    const = lambda shape: pl.BlockSpec(shape, lambda b, i: (0,) * len(shape))
    hist = CONV_K - 1
    shift = np.zeros((hist * rows, BF16_ROWS + rows), np.float32)
    for j in range(hist):
        shift[j * rows + np.arange(rows), BF16_ROWS + np.arange(rows) - hist + j] = 1.0
    return pl.pallas_call(
        _dn_body,
        grid=(batch, ns),
        in_specs=[pl.BlockSpec((rows, 3 * width), lambda b, i: (row(b, i), COL_DN_QKV // (3 * width))),
                  pl.BlockSpec((rows, width), lambda b, i: (row(b, i), COL_DN_GATE // width)),
                  pl.BlockSpec((rows, LANES), lambda b, i: (row(b, i), 0)),
                  const((hist * rows, BF16_ROWS + rows)),
                  const((CONV_K, 3 * width)),
                  const((1, LANES)), const((1, LANES)),
                  const((DN_HEADS, 1)), const((DN_HEADS, 1)),
                  const((1, DN_HEAD_V))],
        out_specs=pl.BlockSpec((rows, width), lambda b, i: (row(b, i), 0)),
        out_shape=jax.ShapeDtypeStruct((t, width), BF16),
        scratch_shapes=[pltpu.VMEM((BF16_ROWS + rows, 3 * width), BF16),
                        pltpu.VMEM((DN_HEADS, DN_HEAD_K, DN_HEAD_V), F32)],
        compiler_params=_cparams(("arbitrary", "arbitrary")),
        name="dn",
    )(proj, proj, ab, jnp.asarray(shift, BF16), conv_w, lane_pad(a_log), lane_pad(dt_bias),
      a_log.reshape(-1, 1), dt_bias.reshape(-1, 1), norm_w.reshape(1, -1))


def _mix_out_body(h_ref, g_ref, a_ref, d_ref, wa_ref, wd_ref, o_ref):
    mixed = jnp.dot(a_ref[...], wa_ref[...], preferred_element_type=F32)
    mixed = mixed + jnp.dot(d_ref[...], wd_ref[...], preferred_element_type=F32)
    o_ref[...] = h_ref[...] + g_ref[0] * mixed


def _mix_out(h, mod3, seq, attn_out, dn_out, w_out):
    tm = MIX_IN_TM
    t, d = h.shape
    assert ATTN_WIDTH == DN_WIDTH
    resident = pl.Buffered(1)
    return pl.pallas_call(
        _mix_out_body,
        grid=(t // tm,),
        in_specs=[pl.BlockSpec((tm, d), lambda i: (i, 0)),
                  pl.BlockSpec((1, 1, d), lambda i: (((i * tm) // seq) * N_MOD + 5, 0, 0)),
                  pl.BlockSpec((tm, ATTN_WIDTH), lambda i: (i, 0)),
                  pl.BlockSpec((tm, DN_WIDTH), lambda i: (i, 0)),
                  pl.BlockSpec((ATTN_WIDTH, d), lambda i: (0, 0), pipeline_mode=resident),
                  pl.BlockSpec((DN_WIDTH, d), lambda i: (1, 0), pipeline_mode=resident)],
        out_specs=pl.BlockSpec((tm, d), lambda i: (i, 0)),
        out_shape=jax.ShapeDtypeStruct((t, d), F32),
        compiler_params=_cparams(("arbitrary",), FFN_VMEM_LIMIT),
        name="mix_out",
    )(h, mod3, attn_out, dn_out, w_out, w_out)


def kernel(x, c, positions, ada_w, ada_b, norm_ffn1, ffn1_w_gate, ffn1_w_up, ffn1_w_down, norm_mix, w_in, conv_w, a_log, dt_bias, attn_sinks, dn_norm_w, w_out, norm_ffn2, ffn2_w_gate, ffn2_w_up, ffn2_w_down, final_norm):
    batch, seq, d = x.shape
    depth = ada_w.shape[0]
    assert depth >= 1 and seq % ATTN_BLOCK == 0 and seq % TOKEN_TILE == 0
    assert seq % (DN_CHUNK * DN_STEP_CHUNKS) == 0
    t = batch * seq
    h = x.reshape(t, d)
    assert seq % FFN_TM == 0 and seq % MIX_IN_TM == 0 and COL_DN_AB % MIX_IN_TN == 0
    o_dq = ATTN_WIDTH + 2 * ATTN_KV_WIDTH
    o_ab = o_dq + 4 * DN_WIDTH
    for l in range(depth):
        mod3 = _adaln(c, ada_w[l], ada_b[l]).reshape(batch * N_MOD, 1, d)
        h = _ffn(h, mod3, seq, 0, norm_ffn1[l], ffn1_w_gate[l], ffn1_w_up[l], ffn1_w_down[l],
                 final_norm, False)
        w = w_in[l]
        w_main = jnp.concatenate([w[:, o_dq:o_ab].astype(BF16), w[:, :o_dq].astype(BF16)], axis=1)
        w_ab = jnp.pad(w[:, o_ab:].astype(BF16), ((0, 0), (0, LANES - 2 * DN_HEADS)))
        proj, ab = _mix_in(h, mod3, seq, norm_mix[l], w_main, w_ab)
        attn_out = _attention(proj, positions, attn_sinks[l], batch, seq)
        dn_out = _deltanet(proj, ab, conv_w[l], a_log[l], dt_bias[l], dn_norm_w[l], batch, seq)
        h = _mix_out(h, mod3, seq, attn_out, dn_out, w_out[l].astype(BF16))
        h = _ffn(h, mod3, seq, 6, norm_ffn2[l], ffn2_w_gate[l], ffn2_w_up[l], ffn2_w_down[l],
                 final_norm, l == depth - 1)
    return h.reshape(batch, seq, d)
```
